```python
import jax, jax.numpy as jnp
from jax import lax
import numpy as np

D_MODEL = 1024
BATCH = 4
SEQ = 8192
DEPTH = 1

HEAD_DIM = 64
SWA_Q_HEADS = 8
SWA_KV_HEADS = 2
SWA_WINDOW = 128
FOX_HEADS = 8
BLOCK = 128
ROPE_THETA = 10000.0
N_EXPERTS = 32
TOP_K = 4
D_FF_EXPERT = D_MODEL
SWIGLU_LIMIT = 7.0
SWIGLU_ALPHA = 1.702
EPS = 1e-6
NEG = -1e30
ATTN_SCALE = HEAD_DIM ** -0.5

SWA_Q_W = SWA_Q_HEADS * HEAD_DIM
SWA_KV_W = SWA_KV_HEADS * HEAD_DIM
FOX_W = FOX_HEADS * HEAD_DIM
IN_SPLITS = (SWA_Q_W, SWA_KV_W, SWA_KV_W, FOX_W, FOX_W, FOX_W, FOX_HEADS, 2 * D_MODEL)
IN_W = SWA_Q_W + 2 * SWA_KV_W + 3 * FOX_W + FOX_HEADS + 2 * D_MODEL

kernel_name = "hybrid_swa_sink_fox_moe_block"


def rms_norm(x, g):
    xf = x.astype(jnp.float32)
    y = xf * lax.rsqrt(jnp.mean(xf * xf, axis=-1, keepdims=True) + EPS)
    return (y * g.astype(jnp.float32)).astype(x.dtype)


def rope(x, positions):
    half = HEAD_DIM // 2
    inv = ROPE_THETA ** (-jnp.arange(half, dtype=jnp.float32) / half)
    ang = positions.astype(jnp.float32)[..., None] * inv
    cos = jnp.cos(ang)[:, :, None, :]
    sin = jnp.sin(ang)[:, :, None, :]
    xf = x.astype(jnp.float32)
    x1, x2 = xf[..., :half], xf[..., half:]
    return jnp.concatenate([x1 * cos - x2 * sin, x2 * cos + x1 * sin], axis=-1).astype(x.dtype)


def split_cols(p):
    idx, acc = [], 0
    for w in IN_SPLITS[:-1]:
        acc += w
        idx.append(acc)
    return jnp.split(p, idx, axis=-1)


def swa_sink_attention(q, k, v, sinks):
    B, S = q.shape[0], q.shape[1]
    nb = S // BLOCK
    G = SWA_Q_HEADS // SWA_KV_HEADS
    qb = q.reshape(B, nb, BLOCK, SWA_KV_HEADS, G, HEAD_DIM)
    kb = k.reshape(B, nb, BLOCK, SWA_KV_HEADS, HEAD_DIM)
    vb = v.reshape(B, nb, BLOCK, SWA_KV_HEADS, HEAD_DIM)

    def with_prev(t):
        prev = jnp.pad(t[:, :-1], ((0, 0), (1, 0), (0, 0), (0, 0), (0, 0)))
        return jnp.concatenate([prev, t], axis=2)

    kk, vv = with_prev(kb), with_prev(vb)
    s = jnp.einsum('bnqhgd,bnkhd->bnhgqk', qb, kk).astype(jnp.float32) * ATTN_SCALE
    qi = jnp.arange(BLOCK)[:, None]
    kj = jnp.arange(2 * BLOCK)[None, :]
    delta = qi + BLOCK - kj
    blk = jnp.arange(nb)[:, None, None]
    valid = (delta >= 0) & (delta < SWA_WINDOW) & (blk * BLOCK + kj - BLOCK >= 0)
    s = jnp.where(valid[None, :, None, None], s, NEG)
    sink = sinks.astype(jnp.float32).reshape(1, 1, SWA_KV_HEADS, G, 1, 1)
    m = jnp.maximum(jnp.max(s, axis=-1, keepdims=True), sink)
    p = jnp.exp(s - m)
    p = p / (jnp.sum(p, axis=-1, keepdims=True) + jnp.exp(sink - m))
    o = jnp.einsum('bnhgqk,bnkhd->bnqhgd', p.astype(v.dtype), vv)
    return o.reshape(B, S, SWA_Q_W)


def forgetting_attention(q, k, v, log_f):
    B, S = q.shape[0], q.shape[1]
    nb = S // BLOCK
    c = jnp.cumsum(log_f, axis=1).transpose(0, 2, 1)
    qb = q.reshape(B, nb, BLOCK, FOX_HEADS, HEAD_DIM).transpose(1, 0, 2, 3, 4)
    cq = c.reshape(B, FOX_HEADS, nb, BLOCK).transpose(2, 0, 1, 3)
    k_pos = jnp.arange(S)

    def block(args):
        q_blk, c_blk, n = args
        s = jnp.einsum('bqhd,bkhd->bhqk', q_blk, k).astype(jnp.float32) * ATTN_SCALE
        s = s + c_blk[..., None] - c[:, :, None, :]
        q_pos = n * BLOCK + jnp.arange(BLOCK)
        s = jnp.where((k_pos[None, :] <= q_pos[:, None])[None, None], s, NEG)
        p = jax.nn.softmax(s, axis=-1).astype(v.dtype)
        return jnp.einsum('bhqk,bkhd->bqhd', p, v)

    o = lax.map(block, (qb, cq, jnp.arange(nb)))
    return o.transpose(1, 0, 2, 3, 4).reshape(B, S, FOX_W)


def moe_ffn(h, w_router, b_router, w_gate_up, b_gate_up, w_down, b_down):
    B, S, D = h.shape
    N = B * S
    t = h.reshape(N, D)
    logits = (t @ w_router).astype(jnp.float32) + b_router.astype(jnp.float32)
    top_val, top_idx = lax.top_k(logits, TOP_K)
    gate = jax.nn.softmax(top_val, axis=-1)
    e_flat = top_idx.reshape(-1).astype(jnp.int32)
    tok_flat = jnp.repeat(jnp.arange(N, dtype=jnp.int32), TOP_K)
    w_flat = gate.reshape(-1)
    order = jnp.argsort(e_flat)
    e_sorted = e_flat[order]
    counts = jnp.zeros((N_EXPERTS,), jnp.int32).at[e_flat].add(1)
    padded = (counts + BLOCK - 1) // BLOCK * BLOCK
    pad_end = jnp.cumsum(padded)
    pad_start = pad_end - padded
    grp_start = jnp.cumsum(counts) - counts
    rank = jnp.arange(N * TOP_K, dtype=jnp.int32) - grp_start[e_sorted]
    dest = pad_start[e_sorted] + rank
    cap = N * TOP_K + N_EXPERTS * BLOCK
    n_blk = cap // BLOCK
    row_tok = jnp.full((cap,), N, jnp.int32).at[dest].set(tok_flat[order])
    row_w = jnp.zeros((cap,), jnp.float32).at[dest].set(w_flat[order])
    blk_expert = jnp.minimum(jnp.searchsorted(pad_end, jnp.arange(n_blk, dtype=jnp.int32) * BLOCK, side='right'),
                             N_EXPERTS - 1).astype(jnp.int32)
    t_pad = jnp.concatenate([t, jnp.zeros((1, D), t.dtype)], axis=0)
    xin = t_pad[row_tok].reshape(n_blk, BLOCK, D)

    def expert_block(args):
        xb, e = args
        gu = xb @ w_gate_up[e] + b_gate_up[e]
        g, u = gu[:, :D_FF_EXPERT], gu[:, D_FF_EXPERT:]
        g = jnp.minimum(g, SWIGLU_LIMIT)
        u = jnp.clip(u, -SWIGLU_LIMIT, SWIGLU_LIMIT)
        a = (u + 1.0) * (g * jax.nn.sigmoid(SWIGLU_ALPHA * g))
        return a @ w_down[e] + b_down[e]

    yb = lax.map(expert_block, (xin, blk_expert)).reshape(cap, D)
    y = jax.ops.segment_sum(yb * row_w[:, None].astype(yb.dtype), row_tok, num_segments=N + 1)[:N]
    return y.reshape(B, S, D)


def hybrid_layer(x, positions, norm1_g, w_in, b_forget, b_gate, q_norm_swa, k_norm_swa, sinks,
                 q_norm_fox, k_norm_fox, w_o_swa, w_o_fox, w_out, norm2_g, w_router, b_router,
                 w_gate_up, b_gate_up, w_down, b_down):
    B, S, D = x.shape
    h = rms_norm(x, norm1_g)
    proj = h @ w_in
    qa, ka, va, qf, kf, vf, fl, gl = split_cols(proj)
    qa = rope(rms_norm(qa.reshape(B, S, SWA_Q_HEADS, HEAD_DIM), q_norm_swa), positions)
    ka = rope(rms_norm(ka.reshape(B, S, SWA_KV_HEADS, HEAD_DIM), k_norm_swa), positions)
    va = va.reshape(B, S, SWA_KV_HEADS, HEAD_DIM)
    out_a = swa_sink_attention(qa, ka, va, sinks)
    qf = rms_norm(qf.reshape(B, S, FOX_HEADS, HEAD_DIM), q_norm_fox)
    kf = rms_norm(kf.reshape(B, S, FOX_HEADS, HEAD_DIM), k_norm_fox)
    vf = vf.reshape(B, S, FOX_HEADS, HEAD_DIM)
    log_f = jax.nn.log_sigmoid((fl + b_forget).astype(jnp.float32))
    out_b = forgetting_attention(qf, kf, vf, log_f)
    gates = jax.nn.sigmoid(gl + b_gate)
    g_a, g_b = gates[..., :D_MODEL], gates[..., D_MODEL:]
    merged = g_a * (out_a @ w_o_swa) + g_b * (out_b @ w_o_fox)
    x = x + merged @ w_out
    h2 = rms_norm(x, norm2_g)
    return x + moe_ffn(h2, w_router, b_router, w_gate_up, b_gate_up, w_down, b_down)


def setup_inputs(seed: int = 0) -> dict:
    key = jax.random.key(seed)
    ks = jax.random.split(key, 24)
    f32 = jnp.float32

    def nrm(k, shape, scale):
        return jax.random.normal(k, shape, f32) * scale

    L = DEPTH
    return {
        "x": nrm(ks[0], (BATCH, SEQ, D_MODEL), 1.0),
        "positions": (jnp.arange(SEQ, dtype=jnp.int32)[None, :]
                      + jax.random.randint(ks[1], (BATCH, 1), 0, 4096, dtype=jnp.int32)),
        "norm1_g": 1.0 + nrm(ks[2], (L, D_MODEL), 0.02),
        "w_in": nrm(ks[3], (L, D_MODEL, IN_W), D_MODEL ** -0.5),
        "b_forget": 3.0 + nrm(ks[4], (L, FOX_HEADS), 0.5),
        "b_gate": nrm(ks[5], (L, 2 * D_MODEL), 0.02),
        "q_norm_swa": 1.0 + nrm(ks[6], (L, HEAD_DIM), 0.02),
        "k_norm_swa": 1.0 + nrm(ks[7], (L, HEAD_DIM), 0.02),
        "sinks": nrm(ks[8], (L, SWA_Q_HEADS), 0.5),
        "q_norm_fox": 1.0 + nrm(ks[9], (L, HEAD_DIM), 0.02),
        "k_norm_fox": 1.0 + nrm(ks[10], (L, HEAD_DIM), 0.02),
        "w_o_swa": nrm(ks[11], (L, SWA_Q_W, D_MODEL), SWA_Q_W ** -0.5),
        "w_o_fox": nrm(ks[12], (L, FOX_W, D_MODEL), FOX_W ** -0.5),
        "w_out": nrm(ks[13], (L, D_MODEL, D_MODEL), D_MODEL ** -0.5),
        "norm2_g": 1.0 + nrm(ks[14], (L, D_MODEL), 0.02),
        "w_router": nrm(ks[15], (L, D_MODEL, N_EXPERTS), D_MODEL ** -0.5),
        "b_router": nrm(ks[16], (L, N_EXPERTS), 0.01),
        "w_gate_up": nrm(ks[17], (L, N_EXPERTS, D_MODEL, 2 * D_FF_EXPERT), D_MODEL ** -0.5),
        "b_gate_up": nrm(ks[18], (L, N_EXPERTS, 2 * D_FF_EXPERT), 0.01),
        "w_down": nrm(ks[19], (L, N_EXPERTS, D_FF_EXPERT, D_MODEL), D_FF_EXPERT ** -0.5),
        "b_down": nrm(ks[20], (L, N_EXPERTS, D_MODEL), 0.01),
    }


def reference(x, positions, norm1_g, w_in, b_forget, b_gate, q_norm_swa, k_norm_swa, sinks,
              q_norm_fox, k_norm_fox, w_o_swa, w_o_fox, w_out, norm2_g, w_router, b_router,
              w_gate_up, b_gate_up, w_down, b_down):
    for l in range(DEPTH):
        x = hybrid_layer(x, positions, norm1_g[l], w_in[l], b_forget[l], b_gate[l],
                         q_norm_swa[l], k_norm_swa[l], sinks[l], q_norm_fox[l], k_norm_fox[l],
                         w_o_swa[l], w_o_fox[l], w_out[l], norm2_g[l], w_router[l], b_router[l],
                         w_gate_up[l], b_gate_up[l], w_down[l], b_down[l])
    return x
```

```python
import functools

import jax
import jax.numpy as jnp
from jax import lax
from jax.experimental import pallas as pl
from jax.experimental.pallas import tpu as pltpu

HEAD_DIM = 64
SWA_Q_HEADS = 8
SWA_KV_HEADS = 2
SWA_BLOCK = 128
FOX_HEADS = 8
ROPE_THETA = 10000.0
N_EXPERTS = 32
TOP_K = 4
SWIGLU_LIMIT = 7.0
SWIGLU_ALPHA = 1.702
EPS = 1e-6
NEG = -1e30
ATTN_SCALE = HEAD_DIM ** -0.5

LANES = 128
VMEM_LIMIT_BYTES = 56 * 1024 * 1024

SWA_Q_W = SWA_Q_HEADS * HEAD_DIM
SWA_KV_W = SWA_KV_HEADS * HEAD_DIM
FOX_W = FOX_HEADS * HEAD_DIM
FOX_AUG_W = FOX_HEADS * LANES

PROJ_TM = 512
SWA_TQ = 512
FOX_TQ = 512
MERGE_TM = 512
DISPATCH_T = 256
EXPERT_TM = 512
COMBINE_T = 256

BF16 = jnp.bfloat16
F32 = jnp.float32


def _params(sem):
    return pltpu.CompilerParams(dimension_semantics=sem, vmem_limit_bytes=VMEM_LIMIT_BYTES)


def _dot(a, b):
    return jnp.dot(a, b, preferred_element_type=F32)


def _dot_nt(a, b):
    return lax.dot_general(a, b, (((1,), (1,)), ((), ())), preferred_element_type=F32)


def _split3(v):
    hi = v.astype(BF16)
    r1 = v - hi.astype(F32)
    mid = r1.astype(BF16)
    lo = (r1 - mid.astype(F32)).astype(BF16)
    return hi, mid, lo


def _lane_iota(shape):
    return lax.broadcasted_iota(jnp.int32, shape, len(shape) - 1)


def _head_norm(t, gain, bd):
    ss = _dot((t * t).astype(BF16), bd)
    return t * lax.rsqrt(ss * (1.0 / HEAD_DIM) + EPS) * gain


def _rope(t, cos, sin_signed, first_half):
    partner = jnp.where(first_half, pltpu.roll(t, LANES - HEAD_DIM // 2, 1), pltpu.roll(t, HEAD_DIM // 2, 1))
    return t * cos + partner * sin_signed


def _in_proj_kernel(tiles_per_batch, x_ref, pos_ref, g1_ref, wqkv_ref, wfl_ref, wgl_ref, bfl_ref, bgl_ref,
                    gqa_ref, gka_ref, gqf_ref, gkf_ref, invf_ref, eq_ref, ek_ref, oneq_ref, onek_ref,
                    qa_ref, ka_ref, va_ref, qf_ref, kf_ref, vf_ref, gate_ref, carry_ref):
    i = pl.program_id(0)
    tm = x_ref.shape[0]

    x = x_ref[...]
    h = x * lax.rsqrt(jnp.mean(x * x, axis=-1, keepdims=True) + EPS) * g1_ref[...]
    h = h.astype(BF16)

    lane = _lane_iota((tm, LANES))
    in_low_head = lane < HEAD_DIM
    first_half = (lane & (HEAD_DIM - 1)) < (HEAD_DIM // 2)
    rr = lax.broadcasted_iota(jnp.int32, (LANES, LANES), 0) < HEAD_DIM
    cc = lax.broadcasted_iota(jnp.int32, (LANES, LANES), 1) < HEAD_DIM
    bd = (rr == cc).astype(BF16)

    ang = pos_ref[...].astype(F32) * invf_ref[...]
    cos = jnp.cos(ang)
    sin_signed = jnp.where(first_half, -jnp.sin(ang), jnp.sin(ang))

    fl = _dot(h, wfl_ref[...]) + bfl_ref[...]
    logf = jnp.minimum(fl, 0.0) - jnp.log(1.0 + jnp.exp(-jnp.abs(fl)))
    logf = jnp.where(lane < FOX_HEADS, logf, 0.0)
    row = lax.broadcasted_iota(jnp.int32, (tm, tm), 0)
    col = lax.broadcasted_iota(jnp.int32, (tm, tm), 1)
    tri = (col <= row).astype(BF16)
    l_hi, l_mid, l_lo = _split3(logf)
    csum = _dot(tri, l_hi) + _dot(tri, l_mid) + _dot(tri, l_lo)

    @pl.when(i % tiles_per_batch == 0)
    def _():
        carry_ref[...] = jnp.zeros_like(carry_ref)

    csum = csum + carry_ref[0:1, :]
    carry_ref[0:1, :] = csum[tm - 1:tm, :]
    c_hi, c_mid, c_lo = _split3(csum)

    for c in range(SWA_Q_W // LANES):
        t = _dot(h, wqkv_ref[:, c * LANES:(c + 1) * LANES])
        t = _rope(_head_norm(t, gqa_ref[:, c * LANES:(c + 1) * LANES], bd), cos, sin_signed, first_half)
        qa_ref[:, c * LANES:(c + 1) * LANES] = t.astype(BF16)
    off = SWA_Q_W
    t = _dot(h, wqkv_ref[:, off:off + LANES])
    t = _rope(_head_norm(t, gka_ref[...], bd), cos, sin_signed, first_half)
    ka_ref[:, 0:LANES] = t.astype(BF16)
    ka_ref[:, LANES:2 * LANES] = pltpu.roll(t, HEAD_DIM, 1).astype(BF16)
    off += SWA_KV_W
    t = _dot(h, wqkv_ref[:, off:off + LANES])
    va_ref[:, 0:LANES] = t.astype(BF16)
    va_ref[:, LANES:2 * LANES] = pltpu.roll(t, HEAD_DIM, 1).astype(BF16)
    off += SWA_KV_W

    aug_q = _dot(c_hi, eq_ref[0]) + _dot(c_mid, eq_ref[1]) + _dot(c_lo, eq_ref[2]) + oneq_ref[...]
    aug_k = _dot(c_hi, ek_ref[0]) + _dot(c_mid, ek_ref[1]) + _dot(c_lo, ek_ref[2]) + onek_ref[...]
    for name_off, gain_ref, aug, out_ref in ((off, gqf_ref, aug_q, qf_ref), (off + FOX_W, gkf_ref, aug_k, kf_ref)):
        for c in range(FOX_W // LANES):
            t = _dot(h, wqkv_ref[:, name_off + c * LANES:name_off + (c + 1) * LANES])
            t = _head_norm(t, gain_ref[:, c * LANES:(c + 1) * LANES], bd)
            even = 2 * c * LANES
            odd = (2 * c + 1) * LANES
            out_ref[:, even:even + LANES] = jnp.where(in_low_head, t, aug[:, even:even + LANES]).astype(BF16)
            out_ref[:, odd:odd + LANES] = jnp.where(in_low_head, aug[:, odd:odd + LANES], t).astype(BF16)
    off += 2 * FOX_W
    for c in range(FOX_W // LANES):
        vf_ref[:, c * LANES:(c + 1) * LANES] = _dot(h, wqkv_ref[:, off + c * LANES:off + (c + 1) * LANES]).astype(BF16)

    gw = 512
    for c in range(gate_ref.shape[1] // gw):
        z = _dot(h, wgl_ref[:, c * gw:(c + 1) * gw]) + bgl_ref[:, c * gw:(c + 1) * gw]
        gate_ref[:, c * gw:(c + 1) * gw] = (1.0 / (1.0 + jnp.exp(-z))).astype(BF16)


def _aug_placement():
    eq = jnp.zeros((3, LANES, FOX_AUG_W), F32)
    ek = jnp.zeros((3, LANES, FOX_AUG_W), F32)
    oneq = jnp.zeros((1, FOX_AUG_W), F32)
    onek = jnp.zeros((1, FOX_AUG_W), F32)
    for h in range(FOX_HEADS):
        base = h * LANES + (HEAD_DIM if h % 2 == 0 else 0)
        for p in range(3):
            eq = eq.at[p, h, base + p].set(1.0)
            ek = ek.at[p, h, base + 3 + p].set(-1.0)
            oneq = oneq.at[0, base + 3 + p].set(1.0)
            onek = onek.at[0, base + p].set(1.0)
    return eq.astype(BF16), ek.astype(BF16), oneq, onek


def _in_proj(x2, pos2, norm1_g, w_in, b_forget, b_gate, q_norm_swa, k_norm_swa, q_norm_fox, k_norm_fox, seq):
    n, d = x2.shape
    tm = PROJ_TM
    n_qkv = SWA_Q_W + 2 * SWA_KV_W + 3 * FOX_W
    wqkv = w_in[:, :n_qkv].astype(BF16)
    wfl = jnp.pad(w_in[:, n_qkv:n_qkv + FOX_HEADS], ((0, 0), (0, LANES - FOX_HEADS))).astype(BF16)
    wgl = w_in[:, n_qkv + FOX_HEADS:].astype(BF16)
    n_gate = wgl.shape[1]
    bfl = jnp.pad(b_forget, (0, LANES - FOX_HEADS)).reshape(1, LANES)
    bgl = b_gate.reshape(1, n_gate)
    gqa = (jnp.tile(q_norm_swa, SWA_Q_HEADS) * ATTN_SCALE).reshape(1, SWA_Q_W)
    gka = jnp.tile(k_norm_swa, SWA_KV_HEADS).reshape(1, SWA_KV_W)
    gqf = (jnp.tile(q_norm_fox, FOX_HEADS) * ATTN_SCALE).reshape(1, FOX_W)
    gkf = jnp.tile(k_norm_fox, FOX_HEADS).reshape(1, FOX_W)
    half = HEAD_DIM // 2
    inv = ROPE_THETA ** (-jnp.arange(half, dtype=F32) / half)
    invf = jnp.tile(inv, LANES // half).reshape(1, LANES)
    eq, ek, oneq, onek = _aug_placement()

    row = lambda w: pl.BlockSpec((tm, w), lambda i: (i, 0))
    full = lambda a: pl.BlockSpec(a.shape, lambda i: (0,) * a.ndim)
    consts = (norm1_g.reshape(1, d), wqkv, wfl, wgl, bfl, bgl, gqa, gka, gqf, gkf, invf, eq, ek, oneq, onek)
    out_widths = (SWA_Q_W, 2 * SWA_KV_W, 2 * SWA_KV_W, FOX_AUG_W, FOX_AUG_W, FOX_W, n_gate)
    return pl.pallas_call(
        functools.partial(_in_proj_kernel, seq // tm),
        grid=(n // tm,),
        in_specs=[row(d), row(1)] + [full(a) for a in consts],
        out_specs=[row(w) for w in out_widths],
        out_shape=[jax.ShapeDtypeStruct((n, w), BF16) for w in out_widths],
        scratch_shapes=[pltpu.VMEM((8, LANES), F32)],
        compiler_params=_params(("arbitrary",)),
        name="in_proj",
    )(x2, pos2, *consts)


def _swa_kernel(sinks_ref, q_ref, kc_ref, kp_ref, vc_ref, vp_ref, o_ref):
    i = pl.program_id(1)
    tq = q_ref.shape[1]
    blk = SWA_BLOCK
    lane = _lane_iota((2 * blk, LANES))
    low = lane < HEAD_DIM
    qi = lax.broadcasted_iota(jnp.int32, (blk, 2 * blk), 0)
    kj = lax.broadcasted_iota(jnp.int32, (blk, 2 * blk), 1)
    cur_ok = (kj >= blk) & (kj - blk <= qi)
    prev_ok = (kj < blk) & (kj > qi)
    zero = jnp.zeros((2 * blk, LANES), BF16)
    group = SWA_Q_HEADS // SWA_KV_HEADS

    for j in range(tq // blk):
        if j == 0:
            k_prev, v_prev = kp_ref[0], vp_ref[0]
            valid = cur_ok | (prev_ok & (i > 0))
        else:
            k_prev, v_prev = kc_ref[0, (j - 1) * blk:j * blk, :], vc_ref[0, (j - 1) * blk:j * blk, :]
            valid = cur_ok | prev_ok
        kwin = jnp.concatenate([k_prev, kc_ref[0, j * blk:(j + 1) * blk, :]], axis=0)
        vwin = jnp.concatenate([v_prev, vc_ref[0, j * blk:(j + 1) * blk, :]], axis=0)
        nat_k, swp_k = kwin[:, :LANES], kwin[:, LANES:]
        nat_v, swp_v = vwin[:, :LANES], vwin[:, LANES:]
        k_var = ((jnp.where(low, nat_k, zero), jnp.where(low, zero, swp_k)),
                 (jnp.where(low, swp_k, zero), jnp.where(low, zero, nat_k)))
        v_var = ((jnp.where(low, nat_v, zero), jnp.where(low, zero, swp_v)),
                 (jnp.where(low, swp_v, zero), jnp.where(low, zero, nat_v)))
        for c in range(SWA_Q_W // LANES):
            qc = q_ref[0, j * blk:(j + 1) * blk, c * LANES:(c + 1) * LANES]
            out = jnp.zeros((blk, LANES), F32)
            for half in range(2):
                head = 2 * c + half
                kv = head // group
                s = jnp.where(valid, _dot_nt(qc, k_var[kv][half]), NEG)
                sink = sinks_ref[head]
                m = jnp.maximum(jnp.max(s, axis=-1, keepdims=True), sink)
                p = jnp.exp(s - m)
                den = jnp.sum(p, axis=-1, keepdims=True) + jnp.exp(sink - m)
                out = out + _dot(p.astype(BF16), v_var[kv][half]) / den
            o_ref[0, j * blk:(j + 1) * blk, c * LANES:(c + 1) * LANES] = out.astype(BF16)


def _swa(qa, ka2, va2, sinks):
    b, s, _ = qa.shape
    tq = SWA_TQ
    per = tq // SWA_BLOCK
    cur = lambda w: pl.BlockSpec((1, tq, w), lambda bi, i, sk: (bi, i, 0))
    prev = lambda w: pl.BlockSpec((1, SWA_BLOCK, w), lambda bi, i, sk: (bi, jnp.maximum(i * per - 1, 0), 0))
    return pl.pallas_call(
        _swa_kernel,
        grid_spec=pltpu.PrefetchScalarGridSpec(
            num_scalar_prefetch=1,
            grid=(b, s // tq),
            in_specs=[cur(SWA_Q_W), cur(2 * SWA_KV_W), prev(2 * SWA_KV_W), cur(2 * SWA_KV_W), prev(2 * SWA_KV_W)],
            out_specs=cur(SWA_Q_W),
        ),
        out_shape=jax.ShapeDtypeStruct((b, s, SWA_Q_W), BF16),
        compiler_params=_params(("arbitrary", "arbitrary")),
        name="swa",
    )(sinks, qa, ka2, ka2, va2, va2)


def _fox_kernel(q_ref, k_ref, v_ref, o_ref, m_ref, l_ref, acc_ref):
    qi = pl.program_id(2)
    tq = q_ref.shape[1]
    tk = tq
    lane = _lane_iota((tk, LANES))
    row = lax.broadcasted_iota(jnp.int32, (tq, tk), 0)
    col = lax.broadcasted_iota(jnp.int32, (tq, tk), 1)
    causal = col <= row
    out = jnp.zeros((tq, LANES), F32)

    for half in range(2):
        q = q_ref[0, :, half * LANES:(half + 1) * LANES]
        in_half = (lane < HEAD_DIM) if half == 0 else (lane >= HEAD_DIM)
        m_ref[...] = jnp.full_like(m_ref, NEG)
        l_ref[...] = jnp.zeros_like(l_ref)
        acc_ref[...] = jnp.zeros_like(acc_ref)

        def step(kb, masked):
            start = pl.multiple_of(kb * tk, tk)
            k = k_ref[0, pl.ds(start, tk), half * LANES:(half + 1) * LANES]
            v = v_ref[0, pl.ds(start, tk), :]
            v = jnp.where(in_half, v, jnp.zeros_like(v))
            s = _dot_nt(q, k)
            if masked:
                s = jnp.where(causal, s, NEG)
            m_prev = m_ref[...]
            m_new = jnp.maximum(m_prev, jnp.max(s, axis=-1, keepdims=True))
            alpha = jnp.exp(m_prev - m_new)
            p = jnp.exp(s - m_new[:, :1])
            l_ref[...] = alpha * l_ref[...] + jnp.sum(p, axis=-1, keepdims=True)
            acc_ref[...] = alpha * acc_ref[...] + _dot(p.astype(BF16), v)
            m_ref[...] = m_new

        def body(kb, carry):
            step(kb, False)
            return carry

        lax.fori_loop(0, qi, body, 0)
        step(qi, True)
        out = out + acc_ref[...] / l_ref[...]
    o_ref[0] = out.astype(BF16)


def _fox(qf, kf, vf):
    b, s, _ = vf.shape
    tq = FOX_TQ
    pairs = FOX_HEADS // 2
    return pl.pallas_call(
        _fox_kernel,
        grid=(b, pairs, s // tq),
        in_specs=[
            pl.BlockSpec((1, tq, 2 * LANES), lambda bi, p, i: (bi, i, p)),
            pl.BlockSpec((1, s, 2 * LANES), lambda bi, p, i: (bi, 0, p)),
            pl.BlockSpec((1, s, LANES), lambda bi, p, i: (bi, 0, p)),
        ],
        out_specs=pl.BlockSpec((1, tq, LANES), lambda bi, p, i: (bi, i, p)),
        out_shape=jax.ShapeDtypeStruct((b, s, FOX_W), BF16),
        scratch_shapes=[pltpu.VMEM((tq, LANES), F32)] * 3,
        compiler_params=_params(("arbitrary", "arbitrary", "arbitrary")),
        name="fox",
    )(qf, kf, vf)


ROW_SUB = 8


def _store_row_tiles(ref, v):
    for j in range(ROW_SUB):
        ref[:, j, :] = v[:, j * LANES:(j + 1) * LANES]


def _load_row_tiles(ref):
    return jnp.concatenate([ref[:, j, :] for j in range(ROW_SUB)], axis=1)


def _merge_kernel(oa_ref, ob_ref, gate_ref, x_ref, woa_ref, wob_ref, wout_ref, g2_ref, wrh_ref, wrl_ref, br_ref,
                  x1_ref, h2_ref, idx_ref, gw_ref, rank_ref, cnt_ref, carry_ref):
    i = pl.program_id(0)
    tm, d = x_ref.shape

    ga = gate_ref[:, :d].astype(F32)
    gb = gate_ref[:, d:].astype(F32)
    merged = ga * _dot(oa_ref[...], woa_ref[...]) + gb * _dot(ob_ref[...], wob_ref[...])
    x1 = x_ref[...] + _dot(merged.astype(BF16), wout_ref[...])
    x1_ref[...] = x1
    h2 = x1 * lax.rsqrt(jnp.mean(x1 * x1, axis=-1, keepdims=True) + EPS) * g2_ref[...]
    _store_row_tiles(h2_ref, h2)

    h_hi = h2.astype(BF16)
    h_lo = (h2 - h_hi.astype(F32)).astype(BF16)
    logits = _dot(h_hi, wrh_ref[...]) + _dot(h_lo, wrh_ref[...]) + _dot(h_hi, wrl_ref[...]) + br_ref[...]
    lane = _lane_iota((tm, LANES))
    lanef = lane.astype(F32)
    logits = jnp.where(lane < N_EXPERTS, logits, NEG)

    sels, vals = [], []
    for _ in range(TOP_K):
        mx = jnp.max(logits, axis=-1, keepdims=True)
        ix = jnp.min(jnp.where(logits == mx, lanef, float(LANES)), axis=-1, keepdims=True)
        sel = lanef == ix
        sels.append((sel, ix))
        vals.append(mx)
        logits = jnp.where(sel, 3.0 * NEG, logits)
    exps = [jnp.exp(v - vals[0]) for v in vals]
    den = exps[0] + exps[1] + exps[2] + exps[3]

    @pl.when(i == 0)
    def _():
        carry_ref[...] = jnp.zeros_like(carry_ref)

    onehot = jnp.zeros((tm, LANES), F32)
    for sel, _ in sels:
        onehot = onehot + sel.astype(F32)
    row = lax.broadcasted_iota(jnp.int32, (tm, tm), 0)
    col = lax.broadcasted_iota(jnp.int32, (tm, tm), 1)
    before = _dot((col < row).astype(BF16), onehot.astype(BF16)) + carry_ref[0:1, :]
    total = carry_ref[0:1, :] + jnp.sum(onehot, axis=0, keepdims=True)
    carry_ref[0:1, :] = total
    cnt_ref[...] = jnp.broadcast_to(total, cnt_ref.shape)

    idx_out = jnp.zeros((tm, LANES), F32)
    gw_out = jnp.zeros((tm, LANES), F32)
    rank_out = jnp.zeros((tm, LANES), F32)
    for k, (sel, ix) in enumerate(sels):
        slot = lane == k
        rank = jnp.sum(jnp.where(sel, before, 0.0), axis=-1, keepdims=True)
        idx_out = jnp.where(slot, ix, idx_out)
        gw_out = jnp.where(slot, exps[k] / den, gw_out)
        rank_out = jnp.where(slot, rank, rank_out)
    idx_ref[...] = idx_out.astype(jnp.int32)
    gw_ref[...] = gw_out
    rank_ref[...] = rank_out.astype(jnp.int32)


def _merge(out_a, out_b, gates, x2, w_o_swa, w_o_fox, w_out, norm2_g, w_router, b_router):
    n, d = x2.shape
    tm = MERGE_TM
    wr = jnp.pad(w_router, ((0, 0), (0, LANES - N_EXPERTS)))
    wrh = wr.astype(BF16)
    wrl = (wr - wrh.astype(F32)).astype(BF16)
    br = jnp.pad(b_router, (0, LANES - N_EXPERTS)).reshape(1, LANES)
    consts = (w_o_swa.astype(BF16), w_o_fox.astype(BF16), w_out.astype(BF16), norm2_g.reshape(1, d), wrh, wrl, br)
    row = lambda w: pl.BlockSpec((tm, w), lambda i: (i, 0))
    full = lambda a: pl.BlockSpec(a.shape, lambda i: (0,) * a.ndim)
    return pl.pallas_call(
        _merge_kernel,
        grid=(n // tm,),
        in_specs=[row(out_a.shape[1]), row(out_b.shape[1]), row(gates.shape[1]), row(d)] + [full(a) for a in consts],
        out_specs=[row(d), pl.BlockSpec((tm, ROW_SUB, LANES), lambda i: (i, 0, 0)), row(LANES), row(LANES), row(LANES),
                   pl.BlockSpec((8, LANES), lambda i: (0, 0))],
        out_shape=[
            jax.ShapeDtypeStruct((n, d), F32),
            jax.ShapeDtypeStruct((n, ROW_SUB, LANES), F32),
            jax.ShapeDtypeStruct((n, LANES), jnp.int32),
            jax.ShapeDtypeStruct((n, LANES), F32),
            jax.ShapeDtypeStruct((n, LANES), jnp.int32),
            jax.ShapeDtypeStruct((8, LANES), F32),
        ],
        scratch_shapes=[pltpu.VMEM((8, LANES), F32)],
        compiler_params=_params(("arbitrary",)),
        name="merge",
    )(out_a, out_b, gates, x2, *consts)


def _idx_copy(dest_ref, idx_ref, sem_ref, tile, slot, per_tile):
    start = pl.multiple_of(tile * per_tile, per_tile)
    return pltpu.make_async_copy(dest_ref.at[pl.ds(start, per_tile)], idx_ref.at[slot], sem_ref.at[slot])


def _dispatch_kernel(tail_start_ref, tail_flag_ref, n_used_ref, dest_ref, h_ref, xin_ref, idx_ref, zeros_ref, idx_sem,
                     row_sem, zero_sem):
    i = pl.program_id(0)
    n_tiles = pl.num_programs(0)
    t = h_ref.shape[0]
    per_tile = t * TOP_K
    slot = i % 2
    tm = zeros_ref.shape[0]
    n_blocks = xin_ref.shape[0] // tm

    def zero_copy(start):
        return pltpu.make_async_copy(zeros_ref, xin_ref.at[pl.ds(start, tm)], zero_sem)

    def unused_block(op):
        def body(blk, carry):
            op(zero_copy(blk * tm))
            return carry
        lax.fori_loop(n_used_ref[0], n_blocks, body, 0)

    @pl.when(i == 0)
    def _():
        zeros_ref[...] = jnp.zeros_like(zeros_ref)
        _idx_copy(dest_ref, idx_ref, idx_sem, 0, 0, per_tile).start()
        for e in range(N_EXPERTS):
            @pl.when(tail_flag_ref[e] > 0)
            def _():
                zero_copy(tail_start_ref[e]).start()
        unused_block(lambda c: c.start())
        for e in range(N_EXPERTS):
            @pl.when(tail_flag_ref[e] > 0)
            def _():
                zero_copy(tail_start_ref[e]).wait()
        unused_block(lambda c: c.wait())

    @pl.when(i + 1 < n_tiles)
    def _():
        _idx_copy(dest_ref, idx_ref, idx_sem, i + 1, 1 - slot, per_tile).start()

    _idx_copy(dest_ref, idx_ref, idx_sem, i, slot, per_tile).wait()

    def row_copy(r, k):
        d = idx_ref[slot, r * TOP_K + k]
        return pltpu.make_async_copy(h_ref.at[r], xin_ref.at[d], row_sem)

    def issue(r, carry):
        for k in range(TOP_K):
            row_copy(r, k).start()
        return carry

    def drain(r, carry):
        for k in range(TOP_K):
            row_copy(r, k).wait()
        return carry

    lax.fori_loop(0, t, issue, 0, unroll=8)
    lax.fori_loop(0, t, drain, 0, unroll=8)


def _dispatch(dest, h2t, tail_start, tail_flag, n_used, cap):
    n = h2t.shape[0]
    t = DISPATCH_T
    tile = (ROW_SUB, LANES)
    return pl.pallas_call(
        _dispatch_kernel,
        grid_spec=pltpu.PrefetchScalarGridSpec(
            num_scalar_prefetch=3,
            grid=(n // t,),
            in_specs=[pl.BlockSpec(memory_space=pl.ANY), pl.BlockSpec((t,) + tile, lambda i, *_: (i, 0, 0))],
            out_specs=pl.BlockSpec(memory_space=pl.ANY),
            scratch_shapes=[
                pltpu.SMEM((2, t * TOP_K), jnp.int32),
                pltpu.VMEM((EXPERT_TM,) + tile, F32),
                pltpu.SemaphoreType.DMA((2,)),
                pltpu.SemaphoreType.DMA,
                pltpu.SemaphoreType.DMA,
            ],
        ),
        out_shape=jax.ShapeDtypeStruct((cap,) + tile, F32),
        compiler_params=_params(("arbitrary",)),
        name="dispatch",
    )(tail_start, tail_flag, n_used, dest, h2t)


def _expert_kernel(blk_expert_ref, n_used_ref, x_ref, wgu_ref, bgu_ref, wd_ref, bd_ref, y_ref):
    i = pl.program_id(0)
    f = wd_ref.shape[1]

    @pl.when(i < n_used_ref[0])
    def _():
        x = _load_row_tiles(x_ref).astype(BF16)
        g = _dot(x, wgu_ref[0, :, :f]) + bgu_ref[0, :, :f]
        u = _dot(x, wgu_ref[0, :, f:]) + bgu_ref[0, :, f:]
        g = jnp.minimum(g, SWIGLU_LIMIT)
        u = jnp.clip(u, -SWIGLU_LIMIT, SWIGLU_LIMIT)
        a = (u + 1.0) * (g * (1.0 / (1.0 + jnp.exp(-SWIGLU_ALPHA * g))))
        _store_row_tiles(y_ref, _dot(a.astype(BF16), wd_ref[0]) + bd_ref[0])

    @pl.when(i >= n_used_ref[0])
    def _():
        y_ref[...] = jnp.zeros_like(y_ref)


def _experts(xin, blk_expert, n_used, w_gate_up, b_gate_up, w_down, b_down):
    cap = xin.shape[0]
    tm = EXPERT_TM
    ne, d, f2 = w_gate_up.shape
    f = f2 // 2
    rows = lambda i, be, nu: (jnp.minimum(i, nu[0] - 1), 0, 0)
    by_expert = lambda i, be, nu: (be[i], 0, 0)
    tile = (ROW_SUB, LANES)
    return pl.pallas_call(
        _expert_kernel,
        grid_spec=pltpu.PrefetchScalarGridSpec(
            num_scalar_prefetch=2,
            grid=(cap // tm,),
            in_specs=[
                pl.BlockSpec((tm,) + tile, rows),
                pl.BlockSpec((1, d, f2), by_expert),
                pl.BlockSpec((1, 1, f2), by_expert),
                pl.BlockSpec((1, f, d), by_expert),
                pl.BlockSpec((1, 1, d), by_expert),
            ],
            out_specs=pl.BlockSpec((tm,) + tile, lambda i, be, nu: (i, 0, 0)),
        ),
        out_shape=jax.ShapeDtypeStruct((cap,) + tile, F32),
        compiler_params=_params(("arbitrary",)),
        name="experts",
    )(blk_expert, n_used, xin, w_gate_up.astype(BF16), b_gate_up.reshape(ne, 1, f2), w_down.astype(BF16),
      b_down.reshape(ne, 1, d))


def _combine_kernel(dest_ref, y_ref, x1_ref, gw_ref, o_ref, idx_ref, rows_ref, idx_sem, row_sem):
    i = pl.program_id(0)
    n_tiles = pl.num_programs(0)
    t = x1_ref.shape[0]
    per_tile = t * TOP_K
    slot = i % 2

    @pl.when(i == 0)
    def _():
        _idx_copy(dest_ref, idx_ref, idx_sem, 0, 0, per_tile).start()

    @pl.when(i + 1 < n_tiles)
    def _():
        _idx_copy(dest_ref, idx_ref, idx_sem, i + 1, 1 - slot, per_tile).start()

    _idx_copy(dest_ref, idx_ref, idx_sem, i, slot, per_tile).wait()

    def row_copy(r, k):
        d = idx_ref[slot, r * TOP_K + k]
        return pltpu.make_async_copy(y_ref.at[d], rows_ref.at[k, r], row_sem)

    def issue(r, carry):
        for k in range(TOP_K):
            row_copy(r, k).start()
        return carry

    def drain(r, carry):
        for k in range(TOP_K):
            row_copy(r, k).wait()
        return carry

    lax.fori_loop(0, t, issue, 0, unroll=8)
    lax.fori_loop(0, t, drain, 0, unroll=8)

    weights = [jnp.broadcast_to(gw_ref[:, k:k + 1], (t, LANES)) for k in range(TOP_K)]
    for j in range(ROW_SUB):
        acc = x1_ref[:, j * LANES:(j + 1) * LANES]
        for k in range(TOP_K):
            acc = acc + weights[k] * rows_ref[k, :, j, :]
        o_ref[:, j * LANES:(j + 1) * LANES] = acc


def _combine(dest, yb, x1, gw):
    n, d = x1.shape
    t = COMBINE_T
    return pl.pallas_call(
        _combine_kernel,
        grid=(n // t,),
        in_specs=[
            pl.BlockSpec(memory_space=pl.ANY),
            pl.BlockSpec(memory_space=pl.ANY),
            pl.BlockSpec((t, d), lambda i: (i, 0)),
            pl.BlockSpec((t, LANES), lambda i: (i, 0)),
        ],
        out_specs=pl.BlockSpec((t, d), lambda i: (i, 0)),
        out_shape=jax.ShapeDtypeStruct((n, d), F32),
        scratch_shapes=[
            pltpu.SMEM((2, t * TOP_K), jnp.int32),
            pltpu.VMEM((TOP_K, t, ROW_SUB, LANES), F32),
            pltpu.SemaphoreType.DMA((2,)),
            pltpu.SemaphoreType.DMA,
        ],
        compiler_params=_params(("arbitrary",)),
        name="combine",
    )(dest, yb, x1, gw)


def _layer(x, positions, norm1_g, w_in, b_forget, b_gate, q_norm_swa, k_norm_swa, sinks, q_norm_fox, k_norm_fox,
           w_o_swa, w_o_fox, w_out, norm2_g, w_router, b_router, w_gate_up, b_gate_up, w_down, b_down):
    b, s, d = x.shape
    n = b * s
    assert s % max(PROJ_TM, SWA_TQ, FOX_TQ) == 0 and n % max(MERGE_TM, DISPATCH_T, COMBINE_T) == 0
    assert d == ROW_SUB * LANES
    x2 = x.reshape(n, d)

    qa, ka2, va2, qf, kf, vf, gates = _in_proj(x2, positions.reshape(n, 1), norm1_g, w_in, b_forget, b_gate,
                                               q_norm_swa, k_norm_swa, q_norm_fox, k_norm_fox, s)
    seq = lambda a: a.reshape(b, s, a.shape[1])
    out_a = _swa(seq(qa), seq(ka2), seq(va2), sinks).reshape(n, SWA_Q_W)
    out_b = _fox(seq(qf), seq(kf), seq(vf)).reshape(n, FOX_W)

    x1, h2t, idx, gw, rank, totals = _merge(out_a, out_b, gates, x2, w_o_swa, w_o_fox, w_out, norm2_g, w_router,
                                            b_router)

    tm = EXPERT_TM
    cap = n * TOP_K + N_EXPERTS * tm
    counts = totals[0, :N_EXPERTS].astype(jnp.int32)
    padded = (counts + tm - 1) // tm * tm
    pad_end = jnp.cumsum(padded)
    pad_start = pad_end - padded
    dest = (pad_start[idx[:, :TOP_K]] + rank[:, :TOP_K]).reshape(n * TOP_K)
    blk_expert = jnp.minimum(jnp.searchsorted(pad_end, jnp.arange(cap // tm, dtype=jnp.int32) * tm, side="right"),
                             N_EXPERTS - 1).astype(jnp.int32)
    n_used = (pad_end[-1:] // tm).astype(jnp.int32)

    xin = _dispatch(dest, h2t, (pad_end - tm).astype(jnp.int32), (padded > 0).astype(jnp.int32), n_used, cap)
    yb = _experts(xin, blk_expert, n_used, w_gate_up, b_gate_up, w_down, b_down)
    return _combine(dest, yb, x1, gw).reshape(b, s, d)


def kernel(x, positions, norm1_g, w_in, b_forget, b_gate, q_norm_swa, k_norm_swa, sinks, q_norm_fox, k_norm_fox,
           w_o_swa, w_o_fox, w_out, norm2_g, w_router, b_router, w_gate_up, b_gate_up, w_down, b_down):
    for l in range(norm1_g.shape[0]):
        x = _layer(x, positions, norm1_g[l], w_in[l], b_forget[l], b_gate[l], q_norm_swa[l], k_norm_swa[l],
                   sinks[l], q_norm_fox[l], k_norm_fox[l], w_o_swa[l], w_o_fox[l], w_out[l], norm2_g[l],
                   w_router[l], b_router[l], w_gate_up[l], b_gate_up[l], w_down[l], b_down[l])
    return x
```

```python
import functools

import jax
import jax.numpy as jnp
import numpy as np
from jax import lax
from jax.experimental import pallas as pl
from jax.experimental.pallas import tpu as pltpu

HEAD_DIM = 64
SWA_Q_HEADS = 8
SWA_KV_HEADS = 2
SWA_BLOCK = 128
FOX_HEADS = 8
ROPE_THETA = 10000.0
N_EXPERTS = 32
TOP_K = 4
SWIGLU_LIMIT = 7.0
SWIGLU_ALPHA = 1.702
EPS = 1e-6
NEG = -1e30
ATTN_SCALE = HEAD_DIM ** -0.5
LOG2E = 1.4426950408889634

LANES = 128
VMEM_LIMIT_BYTES = 56 * 1024 * 1024

SWA_Q_W = SWA_Q_HEADS * HEAD_DIM
SWA_KV_W = SWA_KV_HEADS * HEAD_DIM
FOX_W = FOX_HEADS * HEAD_DIM
FOX_AUG_W = FOX_HEADS * LANES

PROJ_TM = 512
SWA_TQ = 512
FOX_TQ = 1024
FOX_SUB = 512
MERGE_TM = 512
DISPATCH_T = 256
EXPERT_TM = 512
COMBINE_T = 256

BF16 = jnp.bfloat16
F32 = jnp.float32


def _params(sem):
    return pltpu.CompilerParams(dimension_semantics=sem, vmem_limit_bytes=VMEM_LIMIT_BYTES)


def _dot(a, b):
    return jnp.dot(a, b, preferred_element_type=F32)


def _dot_nt(a, b):
    return lax.dot_general(a, b, (((1,), (1,)), ((), ())), preferred_element_type=F32)


def _split3(v):
    hi = v.astype(BF16)
    r1 = v - hi.astype(F32)
    mid = r1.astype(BF16)
    lo = (r1 - mid.astype(F32)).astype(BF16)
    return hi, mid, lo


def _lane_iota(shape):
    return lax.broadcasted_iota(jnp.int32, shape, len(shape) - 1)


def _head_norm(t, gain, bd):
    ss = _dot((t * t).astype(BF16), bd)
    return t * lax.rsqrt(ss * (1.0 / HEAD_DIM) + EPS) * gain


def _rope(t, cos, sin_signed, first_half):
    partner = jnp.where(first_half, pltpu.roll(t, LANES - HEAD_DIM // 2, 1), pltpu.roll(t, HEAD_DIM // 2, 1))
    return t * cos + partner * sin_signed


def _in_proj_kernel(tiles_per_batch, x_ref, pos_ref, g1_ref, wqkv_ref, wfl_ref, wgl_ref, bfl_ref, bgl_ref,
                    gqa_ref, gka_ref, gqf_ref, gkf_ref, invf_ref, eq_ref, ek_ref, oneq_ref, onek_ref,
                    qa_ref, ka_ref, va_ref, qf_ref, kf_ref, vf_ref, gate_ref, carry_ref):
    i = pl.program_id(0)
    tm = x_ref.shape[0]

    x = x_ref[...]
    h = x * lax.rsqrt(jnp.mean(x * x, axis=-1, keepdims=True) + EPS) * g1_ref[...]
    h = h.astype(BF16)

    lane = _lane_iota((tm, LANES))
    in_low_head = lane < HEAD_DIM
    first_half = (lane & (HEAD_DIM - 1)) < (HEAD_DIM // 2)
    rr = lax.broadcasted_iota(jnp.int32, (LANES, LANES), 0) < HEAD_DIM
    cc = lax.broadcasted_iota(jnp.int32, (LANES, LANES), 1) < HEAD_DIM
    bd = (rr == cc).astype(BF16)

    ang = pos_ref[...].astype(F32) * invf_ref[...]
    cos = jnp.cos(ang)
    sin_signed = jnp.where(first_half, -jnp.sin(ang), jnp.sin(ang))

    fl = _dot(h, wfl_ref[...]) + bfl_ref[...]
    logf = jnp.minimum(fl, 0.0) - jnp.log(1.0 + jnp.exp(-jnp.abs(fl)))
    logf = jnp.where(lane < FOX_HEADS, logf, 0.0)
    row = lax.broadcasted_iota(jnp.int32, (tm, tm), 0)
    col = lax.broadcasted_iota(jnp.int32, (tm, tm), 1)
    tri = (col <= row).astype(BF16)
    l_hi, l_mid, l_lo = _split3(logf)
    csum = _dot(tri, l_hi) + _dot(tri, l_mid) + _dot(tri, l_lo)

    @pl.when(i % tiles_per_batch == 0)
    def _():
        carry_ref[...] = jnp.zeros_like(carry_ref)

    csum = csum + carry_ref[0:1, :]
    carry_ref[0:1, :] = csum[tm - 1:tm, :]
    c_hi, c_mid, c_lo = _split3(csum * LOG2E)

    for c in range(SWA_Q_W // LANES):
        t = _dot(h, wqkv_ref[:, c * LANES:(c + 1) * LANES])
        t = _rope(_head_norm(t, gqa_ref[:, c * LANES:(c + 1) * LANES], bd), cos, sin_signed, first_half)
        qa_ref[:, c * LANES:(c + 1) * LANES] = t.astype(BF16)
    off = SWA_Q_W
    t = _dot(h, wqkv_ref[:, off:off + LANES])
    t = _rope(_head_norm(t, gka_ref[...], bd), cos, sin_signed, first_half)
    ka_ref[:, 0:LANES] = t.astype(BF16)
    ka_ref[:, LANES:2 * LANES] = pltpu.roll(t, HEAD_DIM, 1).astype(BF16)
    off += SWA_KV_W
    t = _dot(h, wqkv_ref[:, off:off + LANES])
    va_ref[:, 0:LANES] = t.astype(BF16)
    va_ref[:, LANES:2 * LANES] = pltpu.roll(t, HEAD_DIM, 1).astype(BF16)
    off += SWA_KV_W

    aug_q = _dot(c_hi, eq_ref[0]) + _dot(c_mid, eq_ref[1]) + _dot(c_lo, eq_ref[2]) + oneq_ref[...]
    aug_k = _dot(c_hi, ek_ref[0]) + _dot(c_mid, ek_ref[1]) + _dot(c_lo, ek_ref[2]) + onek_ref[...]
    for name_off, gain_ref, aug, out_ref in ((off, gqf_ref, aug_q, qf_ref), (off + FOX_W, gkf_ref, aug_k, kf_ref)):
        for c in range(FOX_W // LANES):
            t = _dot(h, wqkv_ref[:, name_off + c * LANES:name_off + (c + 1) * LANES])
            t = _head_norm(t, gain_ref[:, c * LANES:(c + 1) * LANES], bd)
            even = 2 * c * LANES
            odd = (2 * c + 1) * LANES
            out_ref[:, even:even + LANES] = jnp.where(in_low_head, t, aug[:, even:even + LANES]).astype(BF16)
            out_ref[:, odd:odd + LANES] = jnp.where(in_low_head, aug[:, odd:odd + LANES], t).astype(BF16)
    off += 2 * FOX_W
    for c in range(FOX_W // LANES):
        vf_ref[:, c * LANES:(c + 1) * LANES] = _dot(h, wqkv_ref[:, off + c * LANES:off + (c + 1) * LANES]).astype(BF16)

    gw = 512
    for c in range(gate_ref.shape[1] // gw):
        z = _dot(h, wgl_ref[:, c * gw:(c + 1) * gw]) + bgl_ref[:, c * gw:(c + 1) * gw]
        gate_ref[:, c * gw:(c + 1) * gw] = (1.0 / (1.0 + jnp.exp(-z))).astype(BF16)


def _aug_placement():
    eq = np.zeros((3, LANES, FOX_AUG_W), np.float32)
    ek = np.zeros((3, LANES, FOX_AUG_W), np.float32)
    oneq = np.zeros((1, FOX_AUG_W), np.float32)
    onek = np.zeros((1, FOX_AUG_W), np.float32)
    for h in range(FOX_HEADS):
        base = h * LANES + (HEAD_DIM if h % 2 == 0 else 0)
        for p in range(3):
            eq[p, h, base + p] = 1.0
            ek[p, h, base + 3 + p] = -1.0
            oneq[0, base + 3 + p] = 1.0
            onek[0, base + p] = 1.0
    return jnp.asarray(eq, BF16), jnp.asarray(ek, BF16), jnp.asarray(oneq), jnp.asarray(onek)


def _in_proj(x2, pos2, norm1_g, w_in, b_forget, b_gate, q_norm_swa, k_norm_swa, q_norm_fox, k_norm_fox, seq):
    n, d = x2.shape
    tm = PROJ_TM
    n_qkv = SWA_Q_W + 2 * SWA_KV_W + 3 * FOX_W
    wqkv = w_in[:, :n_qkv].astype(BF16)
    wfl = jnp.pad(w_in[:, n_qkv:n_qkv + FOX_HEADS], ((0, 0), (0, LANES - FOX_HEADS))).astype(BF16)
    wgl = w_in[:, n_qkv + FOX_HEADS:].astype(BF16)
    n_gate = wgl.shape[1]
    bfl = jnp.pad(b_forget, (0, LANES - FOX_HEADS)).reshape(1, LANES)
    bgl = b_gate.reshape(1, n_gate)
    gqa = (jnp.tile(q_norm_swa, SWA_Q_HEADS) * ATTN_SCALE).reshape(1, SWA_Q_W)
    gka = jnp.tile(k_norm_swa, SWA_KV_HEADS).reshape(1, SWA_KV_W)
    gqf = (jnp.tile(q_norm_fox, FOX_HEADS) * (ATTN_SCALE * LOG2E)).reshape(1, FOX_W)
    gkf = jnp.tile(k_norm_fox, FOX_HEADS).reshape(1, FOX_W)
    half = HEAD_DIM // 2
    inv = ROPE_THETA ** (-jnp.arange(half, dtype=F32) / half)
    invf = jnp.tile(inv, LANES // half).reshape(1, LANES)
    eq, ek, oneq, onek = _aug_placement()

    row = lambda w: pl.BlockSpec((tm, w), lambda i: (i, 0))
    full = lambda a: pl.BlockSpec(a.shape, lambda i: (0,) * a.ndim)
    consts = (norm1_g.reshape(1, d), wqkv, wfl, wgl, bfl, bgl, gqa, gka, gqf, gkf, invf, eq, ek, oneq, onek)
    out_widths = (SWA_Q_W, 2 * SWA_KV_W, 2 * SWA_KV_W, FOX_AUG_W, FOX_AUG_W, FOX_W, n_gate)
    return pl.pallas_call(
        functools.partial(_in_proj_kernel, seq // tm),
        grid=(n // tm,),
        in_specs=[row(d), row(1)] + [full(a) for a in consts],
        out_specs=[row(w) for w in out_widths],
        out_shape=[jax.ShapeDtypeStruct((n, w), BF16) for w in out_widths],
        scratch_shapes=[pltpu.VMEM((8, LANES), F32)],
        compiler_params=_params(("arbitrary",)),
        name="in_proj",
    )(x2, pos2, *consts)


def _swa_kernel(sinks_ref, q_ref, kc_ref, kp_ref, vc_ref, vp_ref, o_ref):
    i = pl.program_id(1)
    tq = q_ref.shape[1]
    blk = SWA_BLOCK
    lane = _lane_iota((2 * blk, LANES))
    low = lane < HEAD_DIM
    qi = lax.broadcasted_iota(jnp.int32, (blk, 2 * blk), 0)
    kj = lax.broadcasted_iota(jnp.int32, (blk, 2 * blk), 1)
    cur_ok = (kj >= blk) & (kj - blk <= qi)
    prev_ok = (kj < blk) & (kj > qi)
    zero = jnp.zeros((2 * blk, LANES), BF16)
    group = SWA_Q_HEADS // SWA_KV_HEADS

    for j in range(tq // blk):
        if j == 0:
            k_prev, v_prev = kp_ref[0], vp_ref[0]
            valid = cur_ok | (prev_ok & (i > 0))
        else:
            k_prev, v_prev = kc_ref[0, (j - 1) * blk:j * blk, :], vc_ref[0, (j - 1) * blk:j * blk, :]
            valid = cur_ok | prev_ok
        kwin = jnp.concatenate([k_prev, kc_ref[0, j * blk:(j + 1) * blk, :]], axis=0)
        vwin = jnp.concatenate([v_prev, vc_ref[0, j * blk:(j + 1) * blk, :]], axis=0)
        nat_k, swp_k = kwin[:, :LANES], kwin[:, LANES:]
        nat_v, swp_v = vwin[:, :LANES], vwin[:, LANES:]
        k_var = ((jnp.where(low, nat_k, zero), jnp.where(low, zero, swp_k)),
                 (jnp.where(low, swp_k, zero), jnp.where(low, zero, nat_k)))
        v_var = ((jnp.where(low, nat_v, zero), jnp.where(low, zero, swp_v)),
                 (jnp.where(low, swp_v, zero), jnp.where(low, zero, nat_v)))
        for c in range(SWA_Q_W // LANES):
            qc = q_ref[0, j * blk:(j + 1) * blk, c * LANES:(c + 1) * LANES]
            out = jnp.zeros((blk, LANES), F32)
            for half in range(2):
                head = 2 * c + half
                kv = head // group
                s = jnp.where(valid, _dot_nt(qc, k_var[kv][half]), NEG)
                sink = sinks_ref[head]
                m = jnp.maximum(jnp.max(s, axis=-1, keepdims=True), sink)
                p = jnp.exp(s - m)
                den = jnp.sum(p, axis=-1, keepdims=True) + jnp.exp(sink - m)
                out = out + _dot(p.astype(BF16), v_var[kv][half]) / den
            o_ref[0, j * blk:(j + 1) * blk, c * LANES:(c + 1) * LANES] = out.astype(BF16)


def _swa(qa, ka2, va2, sinks):
    b, s, _ = qa.shape
    tq = SWA_TQ
    per = tq // SWA_BLOCK
    cur = lambda w: pl.BlockSpec((1, tq, w), lambda bi, i, sk: (bi, i, 0))
    prev = lambda w: pl.BlockSpec((1, SWA_BLOCK, w), lambda bi, i, sk: (bi, jnp.maximum(i * per - 1, 0), 0))
    return pl.pallas_call(
        _swa_kernel,
        grid_spec=pltpu.PrefetchScalarGridSpec(
            num_scalar_prefetch=1,
            grid=(b, s // tq),
            in_specs=[cur(SWA_Q_W), cur(2 * SWA_KV_W), prev(2 * SWA_KV_W), cur(2 * SWA_KV_W), prev(2 * SWA_KV_W)],
            out_specs=cur(SWA_Q_W),
        ),
        out_shape=jax.ShapeDtypeStruct((b, s, SWA_Q_W), BF16),
        compiler_params=_params(("arbitrary", "arbitrary")),
        name="swa",
    )(sinks, qa, ka2, ka2, va2, va2)


def _fox_kernel(q_ref, k_ref, v_ref, o_ref, m_ref, l_ref, acc_ref):
    qi = pl.program_id(2)
    sub = FOX_SUB
    n_sub = q_ref.shape[1] // sub
    tk = sub
    n_chunk = tk // LANES
    row = lax.broadcasted_iota(jnp.int32, (sub, LANES), 0)
    col = lax.broadcasted_iota(jnp.int32, (sub, LANES), 1)
    chains = [(half, sb) for sb in range(n_sub) for half in range(2)]

    m_ref[...] = jnp.full_like(m_ref, NEG)
    l_ref[...] = jnp.zeros_like(l_ref)
    acc_ref[...] = jnp.zeros_like(acc_ref)

    def update(chain, kb, masked):
        half, sb = chains[chain]
        start = pl.multiple_of(kb * tk, tk)
        q = q_ref[0, sb * sub:(sb + 1) * sub, half * LANES:(half + 1) * LANES]
        k = k_ref[0, pl.ds(start, tk), half * LANES:(half + 1) * LANES]
        v = v_ref[0, pl.ds(start, tk), :]
        s = _dot_nt(q, k)
        s = [s[:, c * LANES:(c + 1) * LANES] for c in range(n_chunk)]
        if masked:
            s = [jnp.where(col + c * LANES <= row, s[c], NEG) for c in range(n_chunk)]
        m_prev = m_ref[chain]
        peak = functools.reduce(jnp.maximum, s)
        m_new = jnp.maximum(m_prev, jnp.max(peak, axis=-1, keepdims=True))
        alpha = jnp.exp2(m_prev - m_new)
        p = [jnp.exp2(sc - m_new) for sc in s]
        l_ref[chain] = alpha * l_ref[chain] + jnp.sum(functools.reduce(jnp.add, p), axis=-1, keepdims=True)
        pv = _dot(jnp.concatenate(p, axis=1).astype(BF16), v)
        acc_ref[chain] = alpha * acc_ref[chain] + pv
        m_ref[chain] = m_new

    def body(kb, carry):
        for chain in range(len(chains)):
            update(chain, kb, False)
        return carry

    lax.fori_loop(0, qi * n_sub, body, 0)
    for j in range(n_sub):
        for chain, (half, sb) in enumerate(chains):
            if sb >= j:
                update(chain, qi * n_sub + j, sb == j)

    low = _lane_iota((sub, LANES)) < HEAD_DIM
    for sb in range(n_sub):
        even = acc_ref[2 * sb] / l_ref[2 * sb]
        odd = acc_ref[2 * sb + 1] / l_ref[2 * sb + 1]
        o_ref[0, sb * sub:(sb + 1) * sub, :] = jnp.where(low, even, odd).astype(BF16)


def _fox(qf, kf, vf):
    b, s, _ = vf.shape
    tq = FOX_TQ
    pairs = FOX_HEADS // 2
    n_chain = 2 * (tq // FOX_SUB)
    return pl.pallas_call(
        _fox_kernel,
        grid=(b, pairs, s // tq),
        in_specs=[
            pl.BlockSpec((1, tq, 2 * LANES), lambda bi, p, i: (bi, i, p)),
            pl.BlockSpec((1, s, 2 * LANES), lambda bi, p, i: (bi, 0, p)),
            pl.BlockSpec((1, s, LANES), lambda bi, p, i: (bi, 0, p)),
        ],
        out_specs=pl.BlockSpec((1, tq, LANES), lambda bi, p, i: (bi, i, p)),
        out_shape=jax.ShapeDtypeStruct((b, s, FOX_W), BF16),
        scratch_shapes=[pltpu.VMEM((n_chain, FOX_SUB, LANES), F32)] * 3,
        compiler_params=_params(("arbitrary", "arbitrary", "arbitrary")),
        name="fox",
    )(qf, kf, vf)


ROW_SUB = 8


def _store_row_tiles(ref, v):
    for j in range(ROW_SUB):
        ref[:, j, :] = v[:, j * LANES:(j + 1) * LANES]


def _load_row_tiles(ref):
    return jnp.concatenate([ref[:, j, :] for j in range(ROW_SUB)], axis=1)


def _merge_kernel(oa_ref, ob_ref, gate_ref, x_ref, woa_ref, wob_ref, wout_ref, g2_ref, wrh_ref, wrl_ref, br_ref,
                  x1_ref, h2_ref, idx_ref, gw_ref, rank_ref, cnt_ref, carry_ref):
    i = pl.program_id(0)
    tm, d = x_ref.shape

    ga = gate_ref[:, :d].astype(F32)
    gb = gate_ref[:, d:].astype(F32)
    merged = ga * _dot(oa_ref[...], woa_ref[...]) + gb * _dot(ob_ref[...], wob_ref[...])
    x1 = x_ref[...] + _dot(merged.astype(BF16), wout_ref[...])
    x1_ref[...] = x1
    h2 = x1 * lax.rsqrt(jnp.mean(x1 * x1, axis=-1, keepdims=True) + EPS) * g2_ref[...]
    _store_row_tiles(h2_ref, h2)

    h_hi = h2.astype(BF16)
    h_lo = (h2 - h_hi.astype(F32)).astype(BF16)
    logits = _dot(h_hi, wrh_ref[...]) + _dot(h_lo, wrh_ref[...]) + _dot(h_hi, wrl_ref[...]) + br_ref[...]
    lane = _lane_iota((tm, LANES))
    lanef = lane.astype(F32)
    logits = jnp.where(lane < N_EXPERTS, logits, NEG)

    sels, vals = [], []
    for _ in range(TOP_K):
        mx = jnp.max(logits, axis=-1, keepdims=True)
        ix = jnp.min(jnp.where(logits == mx, lanef, float(LANES)), axis=-1, keepdims=True)
        sel = lanef == ix
        sels.append((sel, ix))
        vals.append(mx)
        logits = jnp.where(sel, 3.0 * NEG, logits)
    exps = [jnp.exp(v - vals[0]) for v in vals]
    den = exps[0] + exps[1] + exps[2] + exps[3]

    @pl.when(i == 0)
    def _():
        carry_ref[...] = jnp.zeros_like(carry_ref)

    onehot = jnp.zeros((tm, LANES), F32)
    for sel, _ in sels:
        onehot = onehot + sel.astype(F32)
    row = lax.broadcasted_iota(jnp.int32, (tm, tm), 0)
    col = lax.broadcasted_iota(jnp.int32, (tm, tm), 1)
    before = _dot((col < row).astype(BF16), onehot.astype(BF16)) + carry_ref[0:1, :]
    total = carry_ref[0:1, :] + jnp.sum(onehot, axis=0, keepdims=True)
    carry_ref[0:1, :] = total
    cnt_ref[...] = jnp.broadcast_to(total, cnt_ref.shape)

    idx_out = jnp.zeros((tm, LANES), F32)
    gw_out = jnp.zeros((tm, LANES), F32)
    rank_out = jnp.zeros((tm, LANES), F32)
    for k, (sel, ix) in enumerate(sels):
        slot = lane == k
        rank = jnp.sum(jnp.where(sel, before, 0.0), axis=-1, keepdims=True)
        idx_out = jnp.where(slot, ix, idx_out)
        gw_out = jnp.where(slot, exps[k] / den, gw_out)
        rank_out = jnp.where(slot, rank, rank_out)
    idx_ref[...] = idx_out.astype(jnp.int32)
    gw_ref[...] = gw_out
    rank_ref[...] = rank_out.astype(jnp.int32)


def _merge(out_a, out_b, gates, x2, w_o_swa, w_o_fox, w_out, norm2_g, w_router, b_router):
    n, d = x2.shape
    tm = MERGE_TM
    wr = jnp.pad(w_router, ((0, 0), (0, LANES - N_EXPERTS)))
    wrh = wr.astype(BF16)
    wrl = (wr - wrh.astype(F32)).astype(BF16)
    br = jnp.pad(b_router, (0, LANES - N_EXPERTS)).reshape(1, LANES)
    consts = (w_o_swa.astype(BF16), w_o_fox.astype(BF16), w_out.astype(BF16), norm2_g.reshape(1, d), wrh, wrl, br)
    row = lambda w: pl.BlockSpec((tm, w), lambda i: (i, 0))
    full = lambda a: pl.BlockSpec(a.shape, lambda i: (0,) * a.ndim)
    return pl.pallas_call(
        _merge_kernel,
        grid=(n // tm,),
        in_specs=[row(out_a.shape[1]), row(out_b.shape[1]), row(gates.shape[1]), row(d)] + [full(a) for a in consts],
        out_specs=[row(d), pl.BlockSpec((tm, ROW_SUB, LANES), lambda i: (i, 0, 0)), row(LANES), row(LANES), row(LANES),
                   pl.BlockSpec((8, LANES), lambda i: (0, 0))],
        out_shape=[
            jax.ShapeDtypeStruct((n, d), F32),
            jax.ShapeDtypeStruct((n, ROW_SUB, LANES), F32),
            jax.ShapeDtypeStruct((n, LANES), jnp.int32),
            jax.ShapeDtypeStruct((n, LANES), F32),
            jax.ShapeDtypeStruct((n, LANES), jnp.int32),
            jax.ShapeDtypeStruct((8, LANES), F32),
        ],
        scratch_shapes=[pltpu.VMEM((8, LANES), F32)],
        compiler_params=_params(("arbitrary",)),
        name="merge",
    )(out_a, out_b, gates, x2, *consts)


def _idx_copy(dest_ref, idx_ref, sem_ref, tile, slot, per_tile):
    start = pl.multiple_of(tile * per_tile, per_tile)
    return pltpu.make_async_copy(dest_ref.at[pl.ds(start, per_tile)], idx_ref.at[slot], sem_ref.at[slot])


def _dispatch_kernel(tail_start_ref, tail_flag_ref, n_used_ref, dest_ref, h_ref, xin_ref, idx_ref, zeros_ref, idx_sem,
                     row_sem, zero_sem):
    i = pl.program_id(0)
    n_tiles = pl.num_programs(0)
    t = h_ref.shape[0]
    per_tile = t * TOP_K
    slot = i % 2
    tm = zeros_ref.shape[0]
    n_blocks = xin_ref.shape[0] // tm

    def zero_copy(start):
        return pltpu.make_async_copy(zeros_ref, xin_ref.at[pl.ds(start, tm)], zero_sem)

    def unused_block(op):
        def body(blk, carry):
            op(zero_copy(blk * tm))
            return carry
        lax.fori_loop(n_used_ref[0], n_blocks, body, 0)

    @pl.when(i == 0)
    def _():
        zeros_ref[...] = jnp.zeros_like(zeros_ref)
        _idx_copy(dest_ref, idx_ref, idx_sem, 0, 0, per_tile).start()
        for e in range(N_EXPERTS):
            @pl.when(tail_flag_ref[e] > 0)
            def _():
                zero_copy(tail_start_ref[e]).start()
        unused_block(lambda c: c.start())
        for e in range(N_EXPERTS):
            @pl.when(tail_flag_ref[e] > 0)
            def _():
                zero_copy(tail_start_ref[e]).wait()
        unused_block(lambda c: c.wait())

    @pl.when(i + 1 < n_tiles)
    def _():
        _idx_copy(dest_ref, idx_ref, idx_sem, i + 1, 1 - slot, per_tile).start()

    _idx_copy(dest_ref, idx_ref, idx_sem, i, slot, per_tile).wait()

    def row_copy(r, k):
        d = idx_ref[slot, r * TOP_K + k]
        return pltpu.make_async_copy(h_ref.at[r], xin_ref.at[d], row_sem)

    def issue(r, carry):
        for k in range(TOP_K):
            row_copy(r, k).start()
        return carry

    def drain(r, carry):
        for k in range(TOP_K):
            row_copy(r, k).wait()
        return carry

    lax.fori_loop(0, t, issue, 0, unroll=8)
    lax.fori_loop(0, t, drain, 0, unroll=8)


def _dispatch(dest, h2t, tail_start, tail_flag, n_used, cap):
    n = h2t.shape[0]
    t = DISPATCH_T
    tile = (ROW_SUB, LANES)
    return pl.pallas_call(
        _dispatch_kernel,
        grid_spec=pltpu.PrefetchScalarGridSpec(
            num_scalar_prefetch=3,
            grid=(n // t,),
            in_specs=[pl.BlockSpec(memory_space=pl.ANY), pl.BlockSpec((t,) + tile, lambda i, *_: (i, 0, 0))],
            out_specs=pl.BlockSpec(memory_space=pl.ANY),
            scratch_shapes=[
                pltpu.SMEM((2, t * TOP_K), jnp.int32),
                pltpu.VMEM((EXPERT_TM,) + tile, F32),
                pltpu.SemaphoreType.DMA((2,)),
                pltpu.SemaphoreType.DMA,
                pltpu.SemaphoreType.DMA,
            ],
        ),
        out_shape=jax.ShapeDtypeStruct((cap,) + tile, F32),
        compiler_params=_params(("arbitrary",)),
        name="dispatch",
    )(tail_start, tail_flag, n_used, dest, h2t)


def _expert_kernel(blk_expert_ref, n_used_ref, x_ref, wgu_ref, bgu_ref, wd_ref, bd_ref, y_ref):
    i = pl.program_id(0)
    f = wd_ref.shape[1]

    @pl.when(i < n_used_ref[0])
    def _():
        x = _load_row_tiles(x_ref).astype(BF16)
        g = _dot(x, wgu_ref[0, :, :f]) + bgu_ref[0, :, :f]
        u = _dot(x, wgu_ref[0, :, f:]) + bgu_ref[0, :, f:]
        g = jnp.minimum(g, SWIGLU_LIMIT)
        u = jnp.clip(u, -SWIGLU_LIMIT, SWIGLU_LIMIT)
        a = (u + 1.0) * (g * (1.0 / (1.0 + jnp.exp(-SWIGLU_ALPHA * g))))
        _store_row_tiles(y_ref, _dot(a.astype(BF16), wd_ref[0]) + bd_ref[0])

    @pl.when(i >= n_used_ref[0])
    def _():
        y_ref[...] = jnp.zeros_like(y_ref)


def _experts(xin, blk_expert, n_used, w_gate_up, b_gate_up, w_down, b_down):
    cap = xin.shape[0]
    tm = EXPERT_TM
    ne, d, f2 = w_gate_up.shape
    f = f2 // 2
    rows = lambda i, be, nu: (jnp.minimum(i, nu[0] - 1), 0, 0)
    by_expert = lambda i, be, nu: (be[i], 0, 0)
    tile = (ROW_SUB, LANES)
    return pl.pallas_call(
        _expert_kernel,
        grid_spec=pltpu.PrefetchScalarGridSpec(
            num_scalar_prefetch=2,
            grid=(cap // tm,),
            in_specs=[
                pl.BlockSpec((tm,) + tile, rows),
                pl.BlockSpec((1, d, f2), by_expert),
                pl.BlockSpec((1, 1, f2), by_expert),
                pl.BlockSpec((1, f, d), by_expert),
                pl.BlockSpec((1, 1, d), by_expert),
            ],
            out_specs=pl.BlockSpec((tm,) + tile, lambda i, be, nu: (i, 0, 0)),
        ),
        out_shape=jax.ShapeDtypeStruct((cap,) + tile, F32),
        compiler_params=_params(("arbitrary",)),
        name="experts",
    )(blk_expert, n_used, xin, w_gate_up.astype(BF16), b_gate_up.reshape(ne, 1, f2), w_down.astype(BF16),
      b_down.reshape(ne, 1, d))


def _combine_kernel(dest_ref, y_ref, x1_ref, gw_ref, o_ref, idx_ref, rows_ref, idx_sem, row_sem):
    i = pl.program_id(0)
    n_tiles = pl.num_programs(0)
    t = x1_ref.shape[0]
    per_tile = t * TOP_K
    slot = i % 2

    @pl.when(i == 0)
    def _():
        _idx_copy(dest_ref, idx_ref, idx_sem, 0, 0, per_tile).start()

    @pl.when(i + 1 < n_tiles)
    def _():
        _idx_copy(dest_ref, idx_ref, idx_sem, i + 1, 1 - slot, per_tile).start()

    _idx_copy(dest_ref, idx_ref, idx_sem, i, slot, per_tile).wait()

    def row_copy(r, k):
        d = idx_ref[slot, r * TOP_K + k]
        return pltpu.make_async_copy(y_ref.at[d], rows_ref.at[k, r], row_sem)

    def issue(r, carry):
        for k in range(TOP_K):
            row_copy(r, k).start()
        return carry

    def drain(r, carry):
        for k in range(TOP_K):
            row_copy(r, k).wait()
        return carry

    lax.fori_loop(0, t, issue, 0, unroll=8)
    lax.fori_loop(0, t, drain, 0, unroll=8)

    weights = [jnp.broadcast_to(gw_ref[:, k:k + 1], (t, LANES)) for k in range(TOP_K)]
    for j in range(ROW_SUB):
        acc = x1_ref[:, j * LANES:(j + 1) * LANES]
        for k in range(TOP_K):
            acc = acc + weights[k] * rows_ref[k, :, j, :]
        o_ref[:, j * LANES:(j + 1) * LANES] = acc


def _combine(dest, yb, x1, gw):
    n, d = x1.shape
    t = COMBINE_T
    return pl.pallas_call(
        _combine_kernel,
        grid=(n // t,),
        in_specs=[
            pl.BlockSpec(memory_space=pl.ANY),
            pl.BlockSpec(memory_space=pl.ANY),
            pl.BlockSpec((t, d), lambda i: (i, 0)),
            pl.BlockSpec((t, LANES), lambda i: (i, 0)),
        ],
        out_specs=pl.BlockSpec((t, d), lambda i: (i, 0)),
        out_shape=jax.ShapeDtypeStruct((n, d), F32),
        scratch_shapes=[
            pltpu.SMEM((2, t * TOP_K), jnp.int32),
            pltpu.VMEM((TOP_K, t, ROW_SUB, LANES), F32),
            pltpu.SemaphoreType.DMA((2,)),
            pltpu.SemaphoreType.DMA,
        ],
        compiler_params=_params(("arbitrary",)),
        name="combine",
    )(dest, yb, x1, gw)


def _layer(x, positions, norm1_g, w_in, b_forget, b_gate, q_norm_swa, k_norm_swa, sinks, q_norm_fox, k_norm_fox,
           w_o_swa, w_o_fox, w_out, norm2_g, w_router, b_router, w_gate_up, b_gate_up, w_down, b_down):
    b, s, d = x.shape
    n = b * s
    assert s % max(PROJ_TM, SWA_TQ, FOX_TQ) == 0 and n % max(MERGE_TM, DISPATCH_T, COMBINE_T) == 0
    assert d == ROW_SUB * LANES
    x2 = x.reshape(n, d)

    qa, ka2, va2, qf, kf, vf, gates = _in_proj(x2, positions.reshape(n, 1), norm1_g, w_in, b_forget, b_gate,
                                               q_norm_swa, k_norm_swa, q_norm_fox, k_norm_fox, s)
    seq = lambda a: a.reshape(b, s, a.shape[1])
    out_a = _swa(seq(qa), seq(ka2), seq(va2), sinks).reshape(n, SWA_Q_W)
    out_b = _fox(seq(qf), seq(kf), seq(vf)).reshape(n, FOX_W)

    x1, h2t, idx, gw, rank, totals = _merge(out_a, out_b, gates, x2, w_o_swa, w_o_fox, w_out, norm2_g, w_router,
                                            b_router)

    tm = EXPERT_TM
    cap = n * TOP_K + N_EXPERTS * tm
    counts = totals[0, :N_EXPERTS].astype(jnp.int32)
    padded = (counts + tm - 1) // tm * tm
    pad_end = jnp.cumsum(padded)
    pad_start = pad_end - padded
    dest = (pad_start[idx[:, :TOP_K]] + rank[:, :TOP_K]).reshape(n * TOP_K)
    blk_start = jnp.arange(cap // tm, dtype=jnp.int32) * tm
    blk_expert = jnp.minimum(jnp.sum(pad_end[None, :] <= blk_start[:, None], axis=1), N_EXPERTS - 1).astype(jnp.int32)
    n_used = (pad_end[-1:] // tm).astype(jnp.int32)

    xin = _dispatch(dest, h2t, (pad_end - tm).astype(jnp.int32), (padded > 0).astype(jnp.int32), n_used, cap)
    yb = _experts(xin, blk_expert, n_used, w_gate_up, b_gate_up, w_down, b_down)
    return _combine(dest, yb, x1, gw).reshape(b, s, d)


def kernel(x, positions, norm1_g, w_in, b_forget, b_gate, q_norm_swa, k_norm_swa, sinks, q_norm_fox, k_norm_fox,
           w_o_swa, w_o_fox, w_out, norm2_g, w_router, b_router, w_gate_up, b_gate_up, w_down, b_down):
    for l in range(norm1_g.shape[0]):
        x = _layer(x, positions, norm1_g[l], w_in[l], b_forget[l], b_gate[l], q_norm_swa[l], k_norm_swa[l],
                   sinks[l], q_norm_fox[l], k_norm_fox[l], w_o_swa[l], w_o_fox[l], w_out[l], norm2_g[l],
                   w_router[l], b_router[l], w_gate_up[l], b_gate_up[l], w_down[l], b_down[l])
    return x
```

```python
import functools

import jax
import jax.numpy as jnp
import numpy as np
from jax import lax
from jax.experimental import pallas as pl
from jax.experimental.pallas import tpu as pltpu

HEAD_DIM = 64
SWA_Q_HEADS = 8
SWA_KV_HEADS = 2
SWA_BLOCK = 128
FOX_HEADS = 8
ROPE_THETA = 10000.0
N_EXPERTS = 32
TOP_K = 4
SWIGLU_LIMIT = 7.0
SWIGLU_ALPHA = 1.702
EPS = 1e-6
NEG = -1e30
ATTN_SCALE = HEAD_DIM ** -0.5
LOG2E = 1.4426950408889634

LANES = 128
VMEM_LIMIT_BYTES = 56 * 1024 * 1024

SWA_Q_W = SWA_Q_HEADS * HEAD_DIM
SWA_KV_W = SWA_KV_HEADS * HEAD_DIM
FOX_W = FOX_HEADS * HEAD_DIM
FOX_AUG_W = FOX_HEADS * LANES

PROJ_TM = 512
SWA_TQ = 512
FOX_TQ = 1024
FOX_SUB = 512
MERGE_TM = 512
DISPATCH_T = 256
EXPERT_TM = 512
EXPERT_FC = 256
COMBINE_T = 256

BF16 = jnp.bfloat16
F32 = jnp.float32


def _params(sem):
    return pltpu.CompilerParams(dimension_semantics=sem, vmem_limit_bytes=VMEM_LIMIT_BYTES)


def _dot(a, b):
    return jnp.dot(a, b, preferred_element_type=F32)


def _dot_nt(a, b):
    return lax.dot_general(a, b, (((1,), (1,)), ((), ())), preferred_element_type=F32)


def _split3(v):
    hi = v.astype(BF16)
    r1 = v - hi.astype(F32)
    mid = r1.astype(BF16)
    lo = (r1 - mid.astype(F32)).astype(BF16)
    return hi, mid, lo


def _lane_iota(shape):
    return lax.broadcasted_iota(jnp.int32, shape, len(shape) - 1)


def _head_norm(t, gain, bd):
    ss = _dot((t * t).astype(BF16), bd)
    return t * lax.rsqrt(ss * (1.0 / HEAD_DIM) + EPS) * gain


def _rope(t, cos, sin_signed, first_half):
    partner = jnp.where(first_half, pltpu.roll(t, LANES - HEAD_DIM // 2, 1), pltpu.roll(t, HEAD_DIM // 2, 1))
    return t * cos + partner * sin_signed


def _in_proj_kernel(tiles_per_batch, x_ref, pos_ref, g1_ref, wqkv_ref, wfl_ref, wgl_ref, bfl_ref, bgl_ref,
                    gqa_ref, gka_ref, gqf_ref, gkf_ref, invf_ref, eq_ref, ek_ref, oneq_ref, onek_ref,
                    qa_ref, ka_ref, va_ref, qf_ref, kf_ref, vf_ref, gate_ref, carry_ref):
    i = pl.program_id(0)
    tm = x_ref.shape[0]

    x = x_ref[...]
    h = x * lax.rsqrt(jnp.mean(x * x, axis=-1, keepdims=True) + EPS) * g1_ref[...]
    h = h.astype(BF16)

    lane = _lane_iota((tm, LANES))
    in_low_head = lane < HEAD_DIM
    first_half = (lane & (HEAD_DIM - 1)) < (HEAD_DIM // 2)
    rr = lax.broadcasted_iota(jnp.int32, (LANES, LANES), 0) < HEAD_DIM
    cc = lax.broadcasted_iota(jnp.int32, (LANES, LANES), 1) < HEAD_DIM
    bd = (rr == cc).astype(BF16)

    ang = pos_ref[...].astype(F32) * invf_ref[...]
    cos = jnp.cos(ang)
    sin_signed = jnp.where(first_half, -jnp.sin(ang), jnp.sin(ang))

    fl = _dot(h, wfl_ref[...]) + bfl_ref[...]
    logf = jnp.minimum(fl, 0.0) - jnp.log(1.0 + jnp.exp(-jnp.abs(fl)))
    logf = jnp.where(lane < FOX_HEADS, logf, 0.0)
    row = lax.broadcasted_iota(jnp.int32, (tm, tm), 0)
    col = lax.broadcasted_iota(jnp.int32, (tm, tm), 1)
    tri = (col <= row).astype(BF16)
    l_hi, l_mid, l_lo = _split3(logf)
    csum = _dot(tri, l_hi) + _dot(tri, l_mid) + _dot(tri, l_lo)

    @pl.when(i % tiles_per_batch == 0)
    def _():
        carry_ref[...] = jnp.zeros_like(carry_ref)

    csum = csum + carry_ref[0:1, :]
    carry_ref[0:1, :] = csum[tm - 1:tm, :]
    c_hi, c_mid, c_lo = _split3(csum * LOG2E)

    for c in range(SWA_Q_W // LANES):
        t = _dot(h, wqkv_ref[:, c * LANES:(c + 1) * LANES])
        t = _rope(_head_norm(t, gqa_ref[:, c * LANES:(c + 1) * LANES], bd), cos, sin_signed, first_half)
        qa_ref[:, c * LANES:(c + 1) * LANES] = t.astype(BF16)
    off = SWA_Q_W
    t = _dot(h, wqkv_ref[:, off:off + LANES])
    t = _rope(_head_norm(t, gka_ref[...], bd), cos, sin_signed, first_half)
    ka_ref[:, 0:LANES] = t.astype(BF16)
    ka_ref[:, LANES:2 * LANES] = pltpu.roll(t, HEAD_DIM, 1).astype(BF16)
    off += SWA_KV_W
    t = _dot(h, wqkv_ref[:, off:off + LANES])
    va_ref[:, 0:LANES] = t.astype(BF16)
    va_ref[:, LANES:2 * LANES] = pltpu.roll(t, HEAD_DIM, 1).astype(BF16)
    off += SWA_KV_W

    aug_q = _dot(c_hi, eq_ref[0]) + _dot(c_mid, eq_ref[1]) + _dot(c_lo, eq_ref[2]) + oneq_ref[...]
    aug_k = _dot(c_hi, ek_ref[0]) + _dot(c_mid, ek_ref[1]) + _dot(c_lo, ek_ref[2]) + onek_ref[...]
    for name_off, gain_ref, aug, out_ref in ((off, gqf_ref, aug_q, qf_ref), (off + FOX_W, gkf_ref, aug_k, kf_ref)):
        for c in range(FOX_W // LANES):
            t = _dot(h, wqkv_ref[:, name_off + c * LANES:name_off + (c + 1) * LANES])
            t = _head_norm(t, gain_ref[:, c * LANES:(c + 1) * LANES], bd)
            even = 2 * c * LANES
            odd = (2 * c + 1) * LANES
            out_ref[:, even:even + LANES] = jnp.where(in_low_head, t, aug[:, even:even + LANES]).astype(BF16)
            out_ref[:, odd:odd + LANES] = jnp.where(in_low_head, aug[:, odd:odd + LANES], t).astype(BF16)
    off += 2 * FOX_W
    for c in range(FOX_W // LANES):
        vf_ref[:, c * LANES:(c + 1) * LANES] = _dot(h, wqkv_ref[:, off + c * LANES:off + (c + 1) * LANES]).astype(BF16)

    gw = 512
    for c in range(gate_ref.shape[1] // gw):
        z = _dot(h, wgl_ref[:, c * gw:(c + 1) * gw]) + bgl_ref[:, c * gw:(c + 1) * gw]
        gate_ref[:, c * gw:(c + 1) * gw] = (1.0 / (1.0 + jnp.exp(-z))).astype(BF16)


def _aug_placement():
    eq = np.zeros((3, LANES, FOX_AUG_W), np.float32)
    ek = np.zeros((3, LANES, FOX_AUG_W), np.float32)
    oneq = np.zeros((1, FOX_AUG_W), np.float32)
    onek = np.zeros((1, FOX_AUG_W), np.float32)
    for h in range(FOX_HEADS):
        base = h * LANES + (HEAD_DIM if h % 2 == 0 else 0)
        for p in range(3):
            eq[p, h, base + p] = 1.0
            ek[p, h, base + 3 + p] = -1.0
            oneq[0, base + 3 + p] = 1.0
            onek[0, base + p] = 1.0
    return jnp.asarray(eq, BF16), jnp.asarray(ek, BF16), jnp.asarray(oneq), jnp.asarray(onek)


def _in_proj(x2, pos2, norm1_g, w_in, b_forget, b_gate, q_norm_swa, k_norm_swa, q_norm_fox, k_norm_fox, seq):
    n, d = x2.shape
    tm = PROJ_TM
    n_qkv = SWA_Q_W + 2 * SWA_KV_W + 3 * FOX_W
    wqkv = w_in[:, :n_qkv].astype(BF16)
    wfl = jnp.pad(w_in[:, n_qkv:n_qkv + FOX_HEADS], ((0, 0), (0, LANES - FOX_HEADS))).astype(BF16)
    wgl = w_in[:, n_qkv + FOX_HEADS:].astype(BF16)
    n_gate = wgl.shape[1]
    bfl = jnp.pad(b_forget, (0, LANES - FOX_HEADS)).reshape(1, LANES)
    bgl = b_gate.reshape(1, n_gate)
    gqa = (jnp.tile(q_norm_swa, SWA_Q_HEADS) * ATTN_SCALE).reshape(1, SWA_Q_W)
    gka = jnp.tile(k_norm_swa, SWA_KV_HEADS).reshape(1, SWA_KV_W)
    gqf = (jnp.tile(q_norm_fox, FOX_HEADS) * (ATTN_SCALE * LOG2E)).reshape(1, FOX_W)
    gkf = jnp.tile(k_norm_fox, FOX_HEADS).reshape(1, FOX_W)
    half = HEAD_DIM // 2
    inv = ROPE_THETA ** (-jnp.arange(half, dtype=F32) / half)
    invf = jnp.tile(inv, LANES // half).reshape(1, LANES)
    eq, ek, oneq, onek = _aug_placement()

    row = lambda w: pl.BlockSpec((tm, w), lambda i: (i, 0))
    full = lambda a: pl.BlockSpec(a.shape, lambda i: (0,) * a.ndim)
    consts = (norm1_g.reshape(1, d), wqkv, wfl, wgl, bfl, bgl, gqa, gka, gqf, gkf, invf, eq, ek, oneq, onek)
    out_widths = (SWA_Q_W, 2 * SWA_KV_W, 2 * SWA_KV_W, FOX_AUG_W, FOX_AUG_W, FOX_W, n_gate)
    return pl.pallas_call(
        functools.partial(_in_proj_kernel, seq // tm),
        grid=(n // tm,),
        in_specs=[row(d), row(1)] + [full(a) for a in consts],
        out_specs=[row(w) for w in out_widths],
        out_shape=[jax.ShapeDtypeStruct((n, w), BF16) for w in out_widths],
        scratch_shapes=[pltpu.VMEM((8, LANES), F32)],
        compiler_params=_params(("arbitrary",)),
        name="in_proj",
    )(x2, pos2, *consts)


def _swa_kernel(sinks_ref, q_ref, kc_ref, kp_ref, vc_ref, vp_ref, o_ref):
    i = pl.program_id(1)
    tq = q_ref.shape[1]
    blk = SWA_BLOCK
    lane = _lane_iota((2 * blk, LANES))
    low = lane < HEAD_DIM
    qi = lax.broadcasted_iota(jnp.int32, (blk, 2 * blk), 0)
    kj = lax.broadcasted_iota(jnp.int32, (blk, 2 * blk), 1)
    cur_ok = (kj >= blk) & (kj - blk <= qi)
    prev_ok = (kj < blk) & (kj > qi)
    zero = jnp.zeros((2 * blk, LANES), BF16)
    group = SWA_Q_HEADS // SWA_KV_HEADS

    for j in range(tq // blk):
        if j == 0:
            k_prev, v_prev = kp_ref[0], vp_ref[0]
            valid = cur_ok | (prev_ok & (i > 0))
        else:
            k_prev, v_prev = kc_ref[0, (j - 1) * blk:j * blk, :], vc_ref[0, (j - 1) * blk:j * blk, :]
            valid = cur_ok | prev_ok
        kwin = jnp.concatenate([k_prev, kc_ref[0, j * blk:(j + 1) * blk, :]], axis=0)
        vwin = jnp.concatenate([v_prev, vc_ref[0, j * blk:(j + 1) * blk, :]], axis=0)
        nat_k, swp_k = kwin[:, :LANES], kwin[:, LANES:]
        nat_v, swp_v = vwin[:, :LANES], vwin[:, LANES:]
        k_var = ((jnp.where(low, nat_k, zero), jnp.where(low, zero, swp_k)),
                 (jnp.where(low, swp_k, zero), jnp.where(low, zero, nat_k)))
        v_var = ((jnp.where(low, nat_v, zero), jnp.where(low, zero, swp_v)),
                 (jnp.where(low, swp_v, zero), jnp.where(low, zero, nat_v)))
        for c in range(SWA_Q_W // LANES):
            qc = q_ref[0, j * blk:(j + 1) * blk, c * LANES:(c + 1) * LANES]
            out = jnp.zeros((blk, LANES), F32)
            for half in range(2):
                head = 2 * c + half
                kv = head // group
                s = jnp.where(valid, _dot_nt(qc, k_var[kv][half]), NEG)
                sink = sinks_ref[head]
                m = jnp.maximum(jnp.max(s, axis=-1, keepdims=True), sink)
                p = jnp.exp(s - m)
                den = jnp.sum(p, axis=-1, keepdims=True) + jnp.exp(sink - m)
                out = out + _dot(p.astype(BF16), v_var[kv][half]) / den
            o_ref[0, j * blk:(j + 1) * blk, c * LANES:(c + 1) * LANES] = out.astype(BF16)


def _swa(qa, ka2, va2, sinks):
    b, s, _ = qa.shape
    tq = SWA_TQ
    per = tq // SWA_BLOCK
    cur = lambda w: pl.BlockSpec((1, tq, w), lambda bi, i, sk: (bi, i, 0))
    prev = lambda w: pl.BlockSpec((1, SWA_BLOCK, w), lambda bi, i, sk: (bi, jnp.maximum(i * per - 1, 0), 0))
    return pl.pallas_call(
        _swa_kernel,
        grid_spec=pltpu.PrefetchScalarGridSpec(
            num_scalar_prefetch=1,
            grid=(b, s // tq),
            in_specs=[cur(SWA_Q_W), cur(2 * SWA_KV_W), prev(2 * SWA_KV_W), cur(2 * SWA_KV_W), prev(2 * SWA_KV_W)],
            out_specs=cur(SWA_Q_W),
        ),
        out_shape=jax.ShapeDtypeStruct((b, s, SWA_Q_W), BF16),
        compiler_params=_params(("arbitrary", "arbitrary")),
        name="swa",
    )(sinks, qa, ka2, ka2, va2, va2)


def _fox_kernel(q_ref, k_ref, v_ref, o_ref, m_ref, l_ref, acc_ref):
    qi = pl.program_id(2)
    sub = FOX_SUB
    n_sub = q_ref.shape[1] // sub
    tk = sub
    n_chunk = tk // LANES
    row = lax.broadcasted_iota(jnp.int32, (sub, LANES), 0)
    col = lax.broadcasted_iota(jnp.int32, (sub, LANES), 1)
    chains = [(half, sb) for sb in range(n_sub) for half in range(2)]

    m_ref[...] = jnp.full_like(m_ref, NEG)
    l_ref[...] = jnp.zeros_like(l_ref)
    acc_ref[...] = jnp.zeros_like(acc_ref)

    def update(chain, kb, masked):
        half, sb = chains[chain]
        start = pl.multiple_of(kb * tk, tk)
        q = q_ref[0, sb * sub:(sb + 1) * sub, half * LANES:(half + 1) * LANES]
        k = k_ref[0, pl.ds(start, tk), half * LANES:(half + 1) * LANES]
        v = v_ref[0, pl.ds(start, tk), :]
        s = _dot_nt(q, k)
        s = [s[:, c * LANES:(c + 1) * LANES] for c in range(n_chunk)]
        if masked:
            s = [jnp.where(col + c * LANES <= row, s[c], NEG) for c in range(n_chunk)]
        m_prev = m_ref[chain]
        peak = functools.reduce(jnp.maximum, s)
        m_new = jnp.maximum(m_prev, jnp.max(peak, axis=-1, keepdims=True))
        alpha = jnp.exp2(m_prev - m_new)
        p = [jnp.exp2(sc - m_new) for sc in s]
        l_ref[chain] = alpha * l_ref[chain] + jnp.sum(functools.reduce(jnp.add, p), axis=-1, keepdims=True)
        pv = _dot(jnp.concatenate(p, axis=1).astype(BF16), v)
        acc_ref[chain] = alpha * acc_ref[chain] + pv
        m_ref[chain] = m_new

    def body(kb, carry):
        for chain in range(len(chains)):
            update(chain, kb, False)
        return carry

    lax.fori_loop(0, qi * n_sub, body, 0)
    for j in range(n_sub):
        for chain, (half, sb) in enumerate(chains):
            if sb >= j:
                update(chain, qi * n_sub + j, sb == j)

    low = _lane_iota((sub, LANES)) < HEAD_DIM
    for sb in range(n_sub):
        even = acc_ref[2 * sb] / l_ref[2 * sb]
        odd = acc_ref[2 * sb + 1] / l_ref[2 * sb + 1]
        o_ref[0, sb * sub:(sb + 1) * sub, :] = jnp.where(low, even, odd).astype(BF16)


def _fox(qf, kf, vf):
    b, s, _ = vf.shape
    tq = FOX_TQ
    pairs = FOX_HEADS // 2
    n_chain = 2 * (tq // FOX_SUB)
    return pl.pallas_call(
        _fox_kernel,
        grid=(b, pairs, s // tq),
        in_specs=[
            pl.BlockSpec((1, tq, 2 * LANES), lambda bi, p, i: (bi, i, p)),
            pl.BlockSpec((1, s, 2 * LANES), lambda bi, p, i: (bi, 0, p)),
            pl.BlockSpec((1, s, LANES), lambda bi, p, i: (bi, 0, p)),
        ],
        out_specs=pl.BlockSpec((1, tq, LANES), lambda bi, p, i: (bi, i, p)),
        out_shape=jax.ShapeDtypeStruct((b, s, FOX_W), BF16),
        scratch_shapes=[pltpu.VMEM((n_chain, FOX_SUB, LANES), F32)] * 3,
        compiler_params=_params(("arbitrary", "arbitrary", "arbitrary")),
        name="fox",
    )(qf, kf, vf)


ROW_SUB = 8


def _group(ref, j, rows, base=0):
    return (pl.ds(base * ROW_SUB + j, rows, stride=ROW_SUB), slice(None))


def _store_row_tiles(ref, v):
    for j in range(ROW_SUB):
        ref[_group(ref, j, v.shape[0])] = v[:, j * LANES:(j + 1) * LANES]


def _load_row_tiles(ref):
    rows = ref.shape[0] // ROW_SUB
    return jnp.concatenate([ref[_group(ref, j, rows)] for j in range(ROW_SUB)], axis=1)


def _row_tile(ref, r):
    return ref.at[pl.ds(pl.multiple_of(r * ROW_SUB, ROW_SUB), ROW_SUB)]


def _merge_kernel(oa_ref, ob_ref, gate_ref, x_ref, woa_ref, wob_ref, wout_ref, g2_ref, wrh_ref, wrl_ref, br_ref,
                  x1_ref, h2_ref, idx_ref, gw_ref, rank_ref, cnt_ref, carry_ref):
    i = pl.program_id(0)
    tm, d = x_ref.shape

    ga = gate_ref[:, :d].astype(F32)
    gb = gate_ref[:, d:].astype(F32)
    merged = ga * _dot(oa_ref[...], woa_ref[...]) + gb * _dot(ob_ref[...], wob_ref[...])
    x1 = x_ref[...] + _dot(merged.astype(BF16), wout_ref[...])
    x1_ref[...] = x1
    h2 = x1 * lax.rsqrt(jnp.mean(x1 * x1, axis=-1, keepdims=True) + EPS) * g2_ref[...]
    _store_row_tiles(h2_ref, h2)

    h_hi = h2.astype(BF16)
    h_lo = (h2 - h_hi.astype(F32)).astype(BF16)
    logits = _dot(h_hi, wrh_ref[...]) + _dot(h_lo, wrh_ref[...]) + _dot(h_hi, wrl_ref[...]) + br_ref[...]
    lane = _lane_iota((tm, LANES))
    lanef = lane.astype(F32)
    logits = jnp.where(lane < N_EXPERTS, logits, NEG)

    sels, vals = [], []
    for _ in range(TOP_K):
        mx = jnp.max(logits, axis=-1, keepdims=True)
        ix = jnp.min(jnp.where(logits == mx, lanef, float(LANES)), axis=-1, keepdims=True)
        sel = lanef == ix
        sels.append((sel, ix))
        vals.append(mx)
        logits = jnp.where(sel, 3.0 * NEG, logits)
    exps = [jnp.exp(v - vals[0]) for v in vals]
    den = exps[0] + exps[1] + exps[2] + exps[3]

    @pl.when(i == 0)
    def _():
        carry_ref[...] = jnp.zeros_like(carry_ref)

    onehot = jnp.zeros((tm, LANES), F32)
    for sel, _ in sels:
        onehot = onehot + sel.astype(F32)
    row = lax.broadcasted_iota(jnp.int32, (tm, tm), 0)
    col = lax.broadcasted_iota(jnp.int32, (tm, tm), 1)
    before = _dot((col < row).astype(BF16), onehot.astype(BF16)) + carry_ref[0:1, :]
    total = carry_ref[0:1, :] + jnp.sum(onehot, axis=0, keepdims=True)
    carry_ref[0:1, :] = total
    cnt_ref[...] = jnp.broadcast_to(total, cnt_ref.shape)

    idx_out = jnp.zeros((tm, LANES), F32)
    gw_out = jnp.zeros((tm, LANES), F32)
    rank_out = jnp.zeros((tm, LANES), F32)
    for k, (sel, ix) in enumerate(sels):
        slot = lane == k
        rank = jnp.sum(jnp.where(sel, before, 0.0), axis=-1, keepdims=True)
        idx_out = jnp.where(slot, ix, idx_out)
        gw_out = jnp.where(slot, exps[k] / den, gw_out)
        rank_out = jnp.where(slot, rank, rank_out)
    idx_ref[...] = idx_out.astype(jnp.int32)
    gw_ref[...] = gw_out
    rank_ref[...] = rank_out.astype(jnp.int32)


def _merge(out_a, out_b, gates, x2, w_o_swa, w_o_fox, w_out, norm2_g, w_router, b_router):
    n, d = x2.shape
    tm = MERGE_TM
    wr = jnp.pad(w_router, ((0, 0), (0, LANES - N_EXPERTS)))
    wrh = wr.astype(BF16)
    wrl = (wr - wrh.astype(F32)).astype(BF16)
    br = jnp.pad(b_router, (0, LANES - N_EXPERTS)).reshape(1, LANES)
    consts = (w_o_swa.astype(BF16), w_o_fox.astype(BF16), w_out.astype(BF16), norm2_g.reshape(1, d), wrh, wrl, br)
    row = lambda w: pl.BlockSpec((tm, w), lambda i: (i, 0))
    full = lambda a: pl.BlockSpec(a.shape, lambda i: (0,) * a.ndim)
    return pl.pallas_call(
        _merge_kernel,
        grid=(n // tm,),
        in_specs=[row(out_a.shape[1]), row(out_b.shape[1]), row(gates.shape[1]), row(d)] + [full(a) for a in consts],
        out_specs=[row(d), pl.BlockSpec((tm * ROW_SUB, LANES), lambda i: (i, 0)), row(LANES), row(LANES), row(LANES),
                   pl.BlockSpec((8, LANES), lambda i: (0, 0))],
        out_shape=[
            jax.ShapeDtypeStruct((n, d), F32),
            jax.ShapeDtypeStruct((n * ROW_SUB, LANES), F32),
            jax.ShapeDtypeStruct((n, LANES), jnp.int32),
            jax.ShapeDtypeStruct((n, LANES), F32),
            jax.ShapeDtypeStruct((n, LANES), jnp.int32),
            jax.ShapeDtypeStruct((8, LANES), F32),
        ],
        scratch_shapes=[pltpu.VMEM((8, LANES), F32)],
        compiler_params=_params(("arbitrary",)),
        name="merge",
    )(out_a, out_b, gates, x2, *consts)


def _idx_copy(dest_ref, idx_ref, sem_ref, tile, slot, per_tile):
    src = dest_ref.at[pl.ds(pl.multiple_of(tile * per_tile, per_tile), per_tile)]
    dst = idx_ref.at[pl.ds(pl.multiple_of(slot * per_tile, per_tile), per_tile)]
    return pltpu.make_async_copy(src, dst, sem_ref.at[slot])


def _dispatch_kernel(tail_start_ref, tail_flag_ref, n_used_ref, dest_ref, h_ref, xin_ref, idx_ref, zeros_ref, idx_sem,
                     row_sem, zero_sem):
    i = pl.program_id(0)
    n_tiles = pl.num_programs(0)
    t = h_ref.shape[0] // ROW_SUB
    per_tile = t * TOP_K
    slot = i % 2
    tm = zeros_ref.shape[0] // ROW_SUB
    n_blocks = xin_ref.shape[0] // zeros_ref.shape[0]

    def zero_copy(start):
        start = pl.multiple_of(start * ROW_SUB, ROW_SUB)
        return pltpu.make_async_copy(zeros_ref, xin_ref.at[pl.ds(start, tm * ROW_SUB)], zero_sem)

    def unused_block(op):
        def body(blk, carry):
            op(zero_copy(blk * tm))
            return carry
        lax.fori_loop(n_used_ref[0], n_blocks, body, 0)

    @pl.when(i == 0)
    def _():
        zeros_ref[...] = jnp.zeros_like(zeros_ref)
        _idx_copy(dest_ref, idx_ref, idx_sem, 0, 0, per_tile).start()
        for e in range(N_EXPERTS):
            @pl.when(tail_flag_ref[e] > 0)
            def _():
                zero_copy(tail_start_ref[e]).start()
        unused_block(lambda c: c.start())
        for e in range(N_EXPERTS):
            @pl.when(tail_flag_ref[e] > 0)
            def _():
                zero_copy(tail_start_ref[e]).wait()
        unused_block(lambda c: c.wait())

    @pl.when(i + 1 < n_tiles)
    def _():
        _idx_copy(dest_ref, idx_ref, idx_sem, i + 1, 1 - slot, per_tile).start()

    _idx_copy(dest_ref, idx_ref, idx_sem, i, slot, per_tile).wait()

    base = slot * per_tile

    def row_copy(r, k):
        return pltpu.make_async_copy(_row_tile(h_ref, r), _row_tile(xin_ref, idx_ref[base + r * TOP_K + k]), row_sem)

    def issue(r, carry):
        for k in range(TOP_K):
            row_copy(r, k).start()
        return carry

    def drain(r, carry):
        for k in range(TOP_K):
            row_copy(r, k).wait()
        return carry

    lax.fori_loop(0, t, issue, 0, unroll=8)
    lax.fori_loop(0, t, drain, 0, unroll=8)


def _dispatch(dest, h2t, tail_start, tail_flag, n_used, cap):
    n = h2t.shape[0] // ROW_SUB
    t = DISPATCH_T
    return pl.pallas_call(
        _dispatch_kernel,
        grid_spec=pltpu.PrefetchScalarGridSpec(
            num_scalar_prefetch=3,
            grid=(n // t,),
            in_specs=[pl.BlockSpec(memory_space=pl.ANY), pl.BlockSpec((t * ROW_SUB, LANES), lambda i, *_: (i, 0))],
            out_specs=pl.BlockSpec(memory_space=pl.ANY),
            scratch_shapes=[
                pltpu.SMEM((2 * t * TOP_K,), jnp.int32),
                pltpu.VMEM((EXPERT_TM * ROW_SUB, LANES), F32),
                pltpu.SemaphoreType.DMA((2,)),
                pltpu.SemaphoreType.DMA,
                pltpu.SemaphoreType.DMA,
            ],
        ),
        out_shape=jax.ShapeDtypeStruct((cap * ROW_SUB, LANES), F32),
        compiler_params=_params(("arbitrary",)),
        name="dispatch",
    )(tail_start, tail_flag, n_used, dest, h2t)


def _expert_kernel(blk_expert_ref, n_used_ref, x_ref, wgu_ref, bgu_ref, wd_ref, bd_ref, y_ref):
    i = pl.program_id(0)
    f = wd_ref.shape[1]

    @pl.when(i < n_used_ref[0])
    def _():
        x = _load_row_tiles(x_ref).astype(BF16)
        y = bd_ref[0]
        for c in range(f // EXPERT_FC):
            lo, hi = c * EXPERT_FC, (c + 1) * EXPERT_FC
            g = _dot(x, wgu_ref[0, :, lo:hi]) + bgu_ref[0, :, lo:hi]
            u = _dot(x, wgu_ref[0, :, f + lo:f + hi]) + bgu_ref[0, :, f + lo:f + hi]
            g = jnp.minimum(g, SWIGLU_LIMIT)
            u = jnp.clip(u, -SWIGLU_LIMIT, SWIGLU_LIMIT)
            a = (u + 1.0) * (g * (1.0 / (1.0 + jnp.exp(-SWIGLU_ALPHA * g))))
            y = y + _dot(a.astype(BF16), wd_ref[0, lo:hi, :])
        _store_row_tiles(y_ref, y)

    @pl.when(i >= n_used_ref[0])
    def _():
        y_ref[...] = jnp.zeros_like(y_ref)


def _experts(xin, blk_expert, n_used, w_gate_up, b_gate_up, w_down, b_down):
    cap = xin.shape[0] // ROW_SUB
    tm = EXPERT_TM
    ne, d, f2 = w_gate_up.shape
    f = f2 // 2
    rows = lambda i, be, nu: (jnp.minimum(i, nu[0] - 1), 0)
    by_expert = lambda i, be, nu: (be[i], 0, 0)
    return pl.pallas_call(
        _expert_kernel,
        grid_spec=pltpu.PrefetchScalarGridSpec(
            num_scalar_prefetch=2,
            grid=(cap // tm,),
            in_specs=[
                pl.BlockSpec((tm * ROW_SUB, LANES), rows),
                pl.BlockSpec((1, d, f2), by_expert),
                pl.BlockSpec((1, 1, f2), by_expert),
                pl.BlockSpec((1, f, d), by_expert),
                pl.BlockSpec((1, 1, d), by_expert),
            ],
            out_specs=pl.BlockSpec((tm * ROW_SUB, LANES), lambda i, be, nu: (i, 0)),
        ),
        out_shape=jax.ShapeDtypeStruct((cap * ROW_SUB, LANES), F32),
        compiler_params=_params(("arbitrary",)),
        name="experts",
    )(blk_expert, n_used, xin, w_gate_up.astype(BF16), b_gate_up.reshape(ne, 1, f2), w_down.astype(BF16),
      b_down.reshape(ne, 1, d))


def _combine_kernel(dest_ref, y_ref, x1_ref, gw_ref, o_ref, idx_ref, rows_ref, idx_sem, row_sem):
    i = pl.program_id(0)
    n_tiles = pl.num_programs(0)
    t = x1_ref.shape[0]
    per_tile = t * TOP_K
    slot = i % 2

    def row_copy(tile_slot, r, k, d):
        dst = _row_tile(rows_ref, (tile_slot * TOP_K + k) * t + r)
        return pltpu.make_async_copy(_row_tile(y_ref, d), dst, row_sem.at[tile_slot])

    def gather(tile_slot):
        base = tile_slot * per_tile

        def issue(r, carry):
            for k in range(TOP_K):
                row_copy(tile_slot, r, k, idx_ref[base + r * TOP_K + k]).start()
            return carry

        lax.fori_loop(0, t, issue, 0, unroll=8)

    @pl.when(i == 0)
    def _():
        _idx_copy(dest_ref, idx_ref, idx_sem, 0, 0, per_tile).start()
        _idx_copy(dest_ref, idx_ref, idx_sem, 0, 0, per_tile).wait()
        gather(0)

        @pl.when(n_tiles > 1)
        def _():
            _idx_copy(dest_ref, idx_ref, idx_sem, 1, 1, per_tile).start()

    @pl.when(i + 1 < n_tiles)
    def _():
        _idx_copy(dest_ref, idx_ref, idx_sem, i + 1, 1 - slot, per_tile).wait()
        gather(1 - slot)

    @pl.when(i + 2 < n_tiles)
    def _():
        _idx_copy(dest_ref, idx_ref, idx_sem, i + 2, slot, per_tile).start()

    def drain(r, carry):
        for k in range(TOP_K):
            row_copy(slot, r, k, 0).wait()
        return carry

    lax.fori_loop(0, t, drain, 0, unroll=8)

    weights = [jnp.broadcast_to(gw_ref[:, k:k + 1], (t, LANES)) for k in range(TOP_K)]
    for j in range(ROW_SUB):
        acc = x1_ref[:, j * LANES:(j + 1) * LANES]
        for k in range(TOP_K):
            acc = acc + weights[k] * rows_ref[_group(rows_ref, j, t, base=(slot * TOP_K + k) * t)]
        o_ref[:, j * LANES:(j + 1) * LANES] = acc


def _combine(dest, yb, x1, gw):
    n, d = x1.shape
    t = COMBINE_T
    return pl.pallas_call(
        _combine_kernel,
        grid=(n // t,),
        in_specs=[
            pl.BlockSpec(memory_space=pl.ANY),
            pl.BlockSpec(memory_space=pl.ANY),
            pl.BlockSpec((t, d), lambda i: (i, 0)),
            pl.BlockSpec((t, LANES), lambda i: (i, 0)),
        ],
        out_specs=pl.BlockSpec((t, d), lambda i: (i, 0)),
        out_shape=jax.ShapeDtypeStruct((n, d), F32),
        scratch_shapes=[
            pltpu.SMEM((2 * t * TOP_K,), jnp.int32),
            pltpu.VMEM((2 * TOP_K * t * ROW_SUB, LANES), F32),
            pltpu.SemaphoreType.DMA((2,)),
            pltpu.SemaphoreType.DMA((2,)),
        ],
        compiler_params=_params(("arbitrary",)),
        name="combine",
    )(dest, yb, x1, gw)


def _layer(x, positions, norm1_g, w_in, b_forget, b_gate, q_norm_swa, k_norm_swa, sinks, q_norm_fox, k_norm_fox,
           w_o_swa, w_o_fox, w_out, norm2_g, w_router, b_router, w_gate_up, b_gate_up, w_down, b_down):
    b, s, d = x.shape
    n = b * s
    assert s % max(PROJ_TM, SWA_TQ, FOX_TQ) == 0 and n % max(MERGE_TM, DISPATCH_T, COMBINE_T) == 0
    assert d == ROW_SUB * LANES
    x2 = x.reshape(n, d)

    qa, ka2, va2, qf, kf, vf, gates = _in_proj(x2, positions.reshape(n, 1), norm1_g, w_in, b_forget, b_gate,
                                               q_norm_swa, k_norm_swa, q_norm_fox, k_norm_fox, s)
    seq = lambda a: a.reshape(b, s, a.shape[1])
    out_a = _swa(seq(qa), seq(ka2), seq(va2), sinks).reshape(n, SWA_Q_W)
    out_b = _fox(seq(qf), seq(kf), seq(vf)).reshape(n, FOX_W)

    x1, h2t, idx, gw, rank, totals = _merge(out_a, out_b, gates, x2, w_o_swa, w_o_fox, w_out, norm2_g, w_router,
                                            b_router)

    tm = EXPERT_TM
    cap = n * TOP_K + N_EXPERTS * tm
    counts = totals[0, :N_EXPERTS].astype(jnp.int32)
    padded = (counts + tm - 1) // tm * tm
    pad_end = jnp.cumsum(padded)
    pad_start = pad_end - padded
    dest = (pad_start[idx[:, :TOP_K]] + rank[:, :TOP_K]).reshape(n * TOP_K)
    blk_start = jnp.arange(cap // tm, dtype=jnp.int32) * tm
    blk_expert = jnp.minimum(jnp.sum(pad_end[None, :] <= blk_start[:, None], axis=1), N_EXPERTS - 1).astype(jnp.int32)
    n_used = (pad_end[-1:] // tm).astype(jnp.int32)

    xin = _dispatch(dest, h2t, (pad_end - tm).astype(jnp.int32), (padded > 0).astype(jnp.int32), n_used, cap)
    yb = _experts(xin, blk_expert, n_used, w_gate_up, b_gate_up, w_down, b_down)
    return _combine(dest, yb, x1, gw).reshape(b, s, d)


def kernel(x, positions, norm1_g, w_in, b_forget, b_gate, q_norm_swa, k_norm_swa, sinks, q_norm_fox, k_norm_fox,
           w_o_swa, w_o_fox, w_out, norm2_g, w_router, b_router, w_gate_up, b_gate_up, w_down, b_down):
    for l in range(norm1_g.shape[0]):
        x = _layer(x, positions, norm1_g[l], w_in[l], b_forget[l], b_gate[l], q_norm_swa[l], k_norm_swa[l],
                   sinks[l], q_norm_fox[l], k_norm_fox[l], w_o_swa[l], w_o_fox[l], w_out[l], norm2_g[l],
                   w_router[l], b_router[l], w_gate_up[l], b_gate_up[l], w_down[l], b_down[l])
    return x
```

```python
import functools

import jax
import jax.numpy as jnp
import numpy as np
from jax import lax
from jax.experimental import pallas as pl
from jax.experimental.pallas import tpu as pltpu

HEAD_DIM = 64
SWA_Q_HEADS = 8
SWA_KV_HEADS = 2
SWA_BLOCK = 128
FOX_HEADS = 8
ROPE_THETA = 10000.0
N_EXPERTS = 32
TOP_K = 4
SWIGLU_LIMIT = 7.0
SWIGLU_ALPHA = 1.702
EPS = 1e-6
NEG = -1e30
ATTN_SCALE = HEAD_DIM ** -0.5
LOG2E = 1.4426950408889634

LANES = 128
VMEM_LIMIT_BYTES = 56 * 1024 * 1024

SWA_Q_W = SWA_Q_HEADS * HEAD_DIM
SWA_KV_W = SWA_KV_HEADS * HEAD_DIM
FOX_W = FOX_HEADS * HEAD_DIM
FOX_AUG_W = FOX_HEADS * LANES

PROJ_TM = 512
SWA_TQ = 512
FOX_TQ = 1024
FOX_SUB = 512
FOX_TK = 512
MERGE_TM = 512
DISPATCH_T = 256
EXPERT_TM = 512
EXPERT_FC = 256
COMBINE_T = 256

BF16 = jnp.bfloat16
F32 = jnp.float32


def _params(sem):
    return pltpu.CompilerParams(dimension_semantics=sem, vmem_limit_bytes=VMEM_LIMIT_BYTES)


def _dot(a, b):
    return jnp.dot(a, b, preferred_element_type=F32)


def _dot_nt(a, b):
    return lax.dot_general(a, b, (((1,), (1,)), ((), ())), preferred_element_type=F32)


def _split3(v):
    hi = v.astype(BF16)
    r1 = v - hi.astype(F32)
    mid = r1.astype(BF16)
    lo = (r1 - mid.astype(F32)).astype(BF16)
    return hi, mid, lo


def _pack3(v):
    hi, mid, lo = _split3(v)
    packed = hi.astype(F32) + pltpu.roll(mid.astype(F32), FOX_HEADS, 1) + pltpu.roll(lo.astype(F32), 2 * FOX_HEADS, 1)
    return packed.astype(BF16)


def _lane_iota(shape):
    return lax.broadcasted_iota(jnp.int32, shape, len(shape) - 1)


def _head_norm(t, gain, bd):
    ss = _dot((t * t).astype(BF16), bd)
    return t * lax.rsqrt(ss * (1.0 / HEAD_DIM) + EPS) * gain


def _rope(t, cos, sin_signed, first_half):
    partner = jnp.where(first_half, pltpu.roll(t, LANES - HEAD_DIM // 2, 1), pltpu.roll(t, HEAD_DIM // 2, 1))
    return t * cos + partner * sin_signed


def _in_proj_kernel(tiles_per_batch, x_ref, pos_ref, g1_ref, wqkv_ref, wfl_ref, wgl_ref, bfl_ref, bgl_ref,
                    gqa_ref, gka_ref, gqf_ref, gkf_ref, invf_ref, eq_ref, ek_ref, oneq_ref, onek_ref,
                    qa_ref, ka_ref, va_ref, qf_ref, kf_ref, vf_ref, gate_ref, carry_ref):
    i = pl.program_id(0)
    tm = x_ref.shape[0]

    x = x_ref[...]
    h = x * lax.rsqrt(jnp.mean(x * x, axis=-1, keepdims=True) + EPS) * g1_ref[...]
    h = h.astype(BF16)

    lane = _lane_iota((tm, LANES))
    in_low_head = lane < HEAD_DIM
    first_half = (lane & (HEAD_DIM - 1)) < (HEAD_DIM // 2)
    shift = HEAD_DIM.bit_length() - 1
    rr = lax.broadcasted_iota(jnp.int32, (2 * LANES, 2 * LANES), 0) >> shift
    cc = lax.broadcasted_iota(jnp.int32, (2 * LANES, 2 * LANES), 1) >> shift
    bd2 = (rr == cc).astype(BF16)
    bd = bd2[:LANES, :LANES]

    ang = pos_ref[...].astype(F32) * invf_ref[...]
    cos = jnp.cos(ang)
    sin_signed = jnp.where(first_half, -jnp.sin(ang), jnp.sin(ang))

    fl = _dot(h, wfl_ref[...]) + bfl_ref[...]
    logf = jnp.minimum(fl, 0.0) - jnp.log(1.0 + jnp.exp(-jnp.abs(fl)))
    logf = jnp.where(lane < FOX_HEADS, logf, 0.0)
    row = lax.broadcasted_iota(jnp.int32, (tm, tm), 0)
    col = lax.broadcasted_iota(jnp.int32, (tm, tm), 1)
    tri = (col <= row).astype(BF16)
    c3 = _dot(tri, _pack3(logf))
    csum = c3 + pltpu.roll(c3, LANES - FOX_HEADS, 1) + pltpu.roll(c3, LANES - 2 * FOX_HEADS, 1)
    csum = jnp.where(lane < FOX_HEADS, csum, 0.0)

    @pl.when(i % tiles_per_batch == 0)
    def _():
        carry_ref[...] = jnp.zeros_like(carry_ref)

    csum = csum + carry_ref[0:1, :]
    carry_ref[0:1, :] = csum[tm - 1:tm, :]
    c_parts = _pack3(csum * LOG2E)

    wide = 2 * LANES
    for c in range(SWA_Q_W // wide):
        t = _dot(h, wqkv_ref[:, c * wide:(c + 1) * wide])
        t = _head_norm(t, gqa_ref[:, c * wide:(c + 1) * wide], bd2)
        for g in range(2):
            lo = c * wide + g * LANES
            qa_ref[:, lo:lo + LANES] = _rope(t[:, g * LANES:(g + 1) * LANES], cos, sin_signed, first_half).astype(BF16)
    off = SWA_Q_W
    t = _dot(h, wqkv_ref[:, off:off + LANES])
    t = _rope(_head_norm(t, gka_ref[...], bd), cos, sin_signed, first_half)
    ka_ref[:, 0:LANES] = t.astype(BF16)
    ka_ref[:, LANES:2 * LANES] = pltpu.roll(t, HEAD_DIM, 1).astype(BF16)
    off += SWA_KV_W
    t = _dot(h, wqkv_ref[:, off:off + LANES])
    va_ref[:, 0:LANES] = t.astype(BF16)
    va_ref[:, LANES:2 * LANES] = pltpu.roll(t, HEAD_DIM, 1).astype(BF16)
    off += SWA_KV_W

    aug_q = _dot(c_parts, eq_ref[...]) + oneq_ref[...]
    aug_k = _dot(c_parts, ek_ref[...]) + onek_ref[...]
    for name_off, gain_ref, aug, out_ref in ((off, gqf_ref, aug_q, qf_ref), (off + FOX_W, gkf_ref, aug_k, kf_ref)):
        for c in range(FOX_W // wide):
            t2 = _dot(h, wqkv_ref[:, name_off + c * wide:name_off + (c + 1) * wide])
            t2 = _head_norm(t2, gain_ref[:, c * wide:(c + 1) * wide], bd2)
            for g in range(2):
                t = t2[:, g * LANES:(g + 1) * LANES]
                even = 2 * (2 * c + g) * LANES
                odd = even + LANES
                out_ref[:, even:even + LANES] = jnp.where(in_low_head, t, aug[:, even:even + LANES]).astype(BF16)
                out_ref[:, odd:odd + LANES] = jnp.where(in_low_head, aug[:, odd:odd + LANES], t).astype(BF16)
    off += 2 * FOX_W
    for c in range(FOX_W // LANES):
        t = _dot(h, wqkv_ref[:, off + c * LANES:off + (c + 1) * LANES])
        vf_ref[:, 2 * c * LANES:(2 * c + 1) * LANES] = jnp.where(in_low_head, t, 1.0).astype(BF16)
        vf_ref[:, (2 * c + 1) * LANES:(2 * c + 2) * LANES] = jnp.where(in_low_head, 1.0, t).astype(BF16)

    gw = 512
    for c in range(gate_ref.shape[1] // gw):
        z = _dot(h, wgl_ref[:, c * gw:(c + 1) * gw]) + bgl_ref[:, c * gw:(c + 1) * gw]
        gate_ref[:, c * gw:(c + 1) * gw] = (1.0 / (1.0 + jnp.exp(-z))).astype(BF16)


def _aug_placement():
    eq = np.zeros((LANES, FOX_AUG_W), np.float32)
    ek = np.zeros((LANES, FOX_AUG_W), np.float32)
    oneq = np.zeros((1, FOX_AUG_W), np.float32)
    onek = np.zeros((1, FOX_AUG_W), np.float32)
    for h in range(FOX_HEADS):
        base = h * LANES + (HEAD_DIM if h % 2 == 0 else 0)
        for p in range(3):
            eq[p * FOX_HEADS + h, base + p] = 1.0
            ek[p * FOX_HEADS + h, base + 3 + p] = -1.0
            oneq[0, base + 3 + p] = 1.0
            onek[0, base + p] = 1.0
    return jnp.asarray(eq, BF16), jnp.asarray(ek, BF16), jnp.asarray(oneq), jnp.asarray(onek)


def _in_proj(x2, pos2, norm1_g, w_in, b_forget, b_gate, q_norm_swa, k_norm_swa, q_norm_fox, k_norm_fox, seq):
    n, d = x2.shape
    tm = PROJ_TM
    n_qkv = SWA_Q_W + 2 * SWA_KV_W + 3 * FOX_W
    wqkv = w_in[:, :n_qkv].astype(BF16)
    wfl = jnp.pad(w_in[:, n_qkv:n_qkv + FOX_HEADS], ((0, 0), (0, LANES - FOX_HEADS))).astype(BF16)
    wgl = w_in[:, n_qkv + FOX_HEADS:].astype(BF16)
    n_gate = wgl.shape[1]
    bfl = jnp.pad(b_forget, (0, LANES - FOX_HEADS)).reshape(1, LANES)
    bgl = b_gate.reshape(1, n_gate)
    gqa = (jnp.tile(q_norm_swa, SWA_Q_HEADS) * ATTN_SCALE).reshape(1, SWA_Q_W)
    gka = jnp.tile(k_norm_swa, SWA_KV_HEADS).reshape(1, SWA_KV_W)
    gqf = (jnp.tile(q_norm_fox, FOX_HEADS) * (ATTN_SCALE * LOG2E)).reshape(1, FOX_W)
    gkf = jnp.tile(k_norm_fox, FOX_HEADS).reshape(1, FOX_W)
    half = HEAD_DIM // 2
    inv = ROPE_THETA ** (-jnp.arange(half, dtype=F32) / half)
    invf = jnp.tile(inv, LANES // half).reshape(1, LANES)
    eq, ek, oneq, onek = _aug_placement()

    row = lambda w: pl.BlockSpec((tm, w), lambda i: (i, 0))
    full = lambda a: pl.BlockSpec(a.shape, lambda i: (0,) * a.ndim)
    consts = (norm1_g.reshape(1, d), wqkv, wfl, wgl, bfl, bgl, gqa, gka, gqf, gkf, invf, eq, ek, oneq, onek)
    out_widths = (SWA_Q_W, 2 * SWA_KV_W, 2 * SWA_KV_W, FOX_AUG_W, FOX_AUG_W, 2 * FOX_W, n_gate)
    return pl.pallas_call(
        functools.partial(_in_proj_kernel, seq // tm),
        grid=(n // tm,),
        in_specs=[row(d), row(1)] + [full(a) for a in consts],
        out_specs=[row(w) for w in out_widths],
        out_shape=[jax.ShapeDtypeStruct((n, w), BF16) for w in out_widths],
        scratch_shapes=[pltpu.VMEM((8, LANES), F32)],
        compiler_params=_params(("arbitrary",)),
        name="in_proj",
    )(x2, pos2, *consts)


def _swa_kernel(sinks_ref, q_ref, kc_ref, kp_ref, vc_ref, vp_ref, o_ref):
    i = pl.program_id(1)
    tq = q_ref.shape[1]
    blk = SWA_BLOCK
    lane = _lane_iota((2 * blk, LANES))
    low = lane < HEAD_DIM
    qi = lax.broadcasted_iota(jnp.int32, (blk, 2 * blk), 0)
    kj = lax.broadcasted_iota(jnp.int32, (blk, 2 * blk), 1)
    cur_ok = (kj >= blk) & (kj - blk <= qi)
    prev_ok = (kj < blk) & (kj > qi)
    zero = jnp.zeros((2 * blk, LANES), BF16)
    group = SWA_Q_HEADS // SWA_KV_HEADS

    for j in range(tq // blk):
        if j == 0:
            k_prev, v_prev = kp_ref[0], vp_ref[0]
            valid = cur_ok | (prev_ok & (i > 0))
        else:
            k_prev, v_prev = kc_ref[0, (j - 1) * blk:j * blk, :], vc_ref[0, (j - 1) * blk:j * blk, :]
            valid = cur_ok | prev_ok
        kwin = jnp.concatenate([k_prev, kc_ref[0, j * blk:(j + 1) * blk, :]], axis=0)
        vwin = jnp.concatenate([v_prev, vc_ref[0, j * blk:(j + 1) * blk, :]], axis=0)
        nat_k, swp_k = kwin[:, :LANES], kwin[:, LANES:]
        nat_v, swp_v = vwin[:, :LANES], vwin[:, LANES:]
        k_var = ((jnp.where(low, nat_k, zero), jnp.where(low, zero, swp_k)),
                 (jnp.where(low, swp_k, zero), jnp.where(low, zero, nat_k)))
        v_var = ((jnp.where(low, nat_v, zero), jnp.where(low, zero, swp_v)),
                 (jnp.where(low, swp_v, zero), jnp.where(low, zero, nat_v)))
        for c in range(SWA_Q_W // LANES):
            qc = q_ref[0, j * blk:(j + 1) * blk, c * LANES:(c + 1) * LANES]
            out = jnp.zeros((blk, LANES), F32)
            for half in range(2):
                head = 2 * c + half
                kv = head // group
                s = jnp.where(valid, _dot_nt(qc, k_var[kv][half]), NEG)
                sink = sinks_ref[head]
                m = jnp.maximum(jnp.max(s, axis=-1, keepdims=True), sink)
                p = jnp.exp(s - m)
                den = jnp.sum(p, axis=-1, keepdims=True) + jnp.exp(sink - m)
                out = out + _dot(p.astype(BF16), v_var[kv][half]) / den
            o_ref[0, j * blk:(j + 1) * blk, c * LANES:(c + 1) * LANES] = out.astype(BF16)


def _swa(qa, ka2, va2, sinks):
    b, s, _ = qa.shape
    tq = SWA_TQ
    per = tq // SWA_BLOCK
    cur = lambda w: pl.BlockSpec((1, tq, w), lambda bi, i, sk: (bi, i, 0))
    prev = lambda w: pl.BlockSpec((1, SWA_BLOCK, w), lambda bi, i, sk: (bi, jnp.maximum(i * per - 1, 0), 0))
    return pl.pallas_call(
        _swa_kernel,
        grid_spec=pltpu.PrefetchScalarGridSpec(
            num_scalar_prefetch=1,
            grid=(b, s // tq),
            in_specs=[cur(SWA_Q_W), cur(2 * SWA_KV_W), prev(2 * SWA_KV_W), cur(2 * SWA_KV_W), prev(2 * SWA_KV_W)],
            out_specs=cur(SWA_Q_W),
        ),
        out_shape=jax.ShapeDtypeStruct((b, s, SWA_Q_W), BF16),
        compiler_params=_params(("arbitrary", "arbitrary")),
        name="swa",
    )(sinks, qa, ka2, ka2, va2, va2)


def _fox_kernel(q_ref, k_ref, v_ref, o_ref, m_ref, acc_ref, s_ref):
    qi = pl.program_id(2)
    sub, tk = FOX_SUB, FOX_TK
    tq = q_ref.shape[1]
    n_sub = tq // sub
    n_chunk = tk // LANES
    row = lax.broadcasted_iota(jnp.int32, (sub, LANES), 0)
    col = lax.broadcasted_iota(jnp.int32, (sub, LANES), 1)
    chains = [(half, sb) for sb in range(n_sub) for half in range(2)]

    m_ref[...] = jnp.full_like(m_ref, NEG)
    acc_ref[...] = jnp.zeros_like(acc_ref)

    def scores(buf, chain, kb):
        half, sb = chains[chain]
        start = pl.multiple_of(kb * tk, tk)
        q = q_ref[0, sb * sub:(sb + 1) * sub, half * LANES:(half + 1) * LANES]
        k = k_ref[0, pl.ds(start, tk), half * LANES:(half + 1) * LANES]
        s_ref[buf, chain] = _dot_nt(q, k)

    def fold(buf, chain, kb, mask_offset):
        half, sb = chains[chain]
        start = pl.multiple_of(kb * tk, tk)
        v = v_ref[0, pl.ds(start, tk), half * LANES:(half + 1) * LANES]
        s = [s_ref[buf, chain, :, c * LANES:(c + 1) * LANES] for c in range(n_chunk)]
        if mask_offset is not None:
            s = [jnp.where(col + (c * LANES + mask_offset) <= row, s[c], NEG) for c in range(n_chunk)]
        m_prev = m_ref[chain]
        peak = functools.reduce(jnp.maximum, s)
        m_new = jnp.maximum(m_prev, jnp.max(peak, axis=-1, keepdims=True))
        alpha = jnp.exp2(m_prev - m_new)
        p = [jnp.exp2(sc - m_new) for sc in s]
        pv = _dot(jnp.concatenate(p, axis=1).astype(BF16), v)
        acc_ref[chain] = alpha * acc_ref[chain] + pv
        m_ref[chain] = m_new

    assert tq // tk == 2
    every = range(len(chains))
    for chain in every:
        scores(0, chain, 0)

    def body(t, carry):
        for chain in every:
            scores(1, chain, 2 * t + 1)
        for chain in every:
            fold(0, chain, 2 * t, None)
        for chain in every:
            scores(0, chain, 2 * t + 2)
        for chain in every:
            fold(1, chain, 2 * t + 1, None)
        return carry

    lax.fori_loop(0, qi, body, 0)

    def visibility(j, sb):
        first_key, first_row = j * tk, sb * sub
        if first_key + tk - 1 <= first_row:
            return True, None
        if first_key <= first_row + sub - 1:
            return True, first_key - first_row
        return False, None

    for chain, (half, sb) in enumerate(chains):
        if visibility(1, sb)[0]:
            scores(1, chain, 2 * qi + 1)
    for j in range(2):
        for chain, (half, sb) in enumerate(chains):
            visible, mask_offset = visibility(j, sb)
            if visible:
                fold(j, chain, 2 * qi + j, mask_offset)

    low = _lane_iota((sub, LANES)) < HEAD_DIM
    for sb in range(n_sub):
        even, odd = acc_ref[2 * sb], acc_ref[2 * sb + 1]
        even = even / pltpu.roll(even, HEAD_DIM, 1)
        odd = odd / pltpu.roll(odd, HEAD_DIM, 1)
        o_ref[0, sb * sub:(sb + 1) * sub, :] = jnp.where(low, even, odd).astype(BF16)


def _fox(qf, kf, vf):
    b, s, _ = qf.shape
    tq = FOX_TQ
    pairs = FOX_HEADS // 2
    n_chain = 2 * (tq // FOX_SUB)
    return pl.pallas_call(
        _fox_kernel,
        grid=(b, pairs, s // tq),
        in_specs=[
            pl.BlockSpec((1, tq, 2 * LANES), lambda bi, p, i: (bi, i, p)),
            pl.BlockSpec((1, s, 2 * LANES), lambda bi, p, i: (bi, 0, p)),
            pl.BlockSpec((1, s, 2 * LANES), lambda bi, p, i: (bi, 0, p)),
        ],
        out_specs=pl.BlockSpec((1, tq, LANES), lambda bi, p, i: (bi, i, p)),
        out_shape=jax.ShapeDtypeStruct((b, s, FOX_W), BF16),
        scratch_shapes=[pltpu.VMEM((n_chain, FOX_SUB, LANES), F32)] * 2
        + [pltpu.VMEM((2, n_chain, FOX_SUB, FOX_TK), F32)],
        compiler_params=_params(("arbitrary", "arbitrary", "arbitrary")),
        name="fox",
    )(qf, kf, vf)


ROW_SUB = 8


def _group(ref, j, rows, base=0):
    return (pl.ds(base * ROW_SUB + j, rows, stride=ROW_SUB), slice(None))


def _store_row_tiles(ref, v):
    for j in range(ROW_SUB):
        ref[_group(ref, j, v.shape[0])] = v[:, j * LANES:(j + 1) * LANES]


def _load_row_tiles(ref):
    rows = ref.shape[0] // ROW_SUB
    return jnp.concatenate([ref[_group(ref, j, rows)] for j in range(ROW_SUB)], axis=1)


def _row_tile(ref, r):
    return ref.at[pl.ds(pl.multiple_of(r * ROW_SUB, ROW_SUB), ROW_SUB)]


def _merge_kernel(oa_ref, ob_ref, gate_ref, x_ref, woa_ref, wob_ref, wout_ref, g2_ref, wrh_ref, wrl_ref, br_ref,
                  x1_ref, h2_ref, idx_ref, gw_ref, rank_ref, cnt_ref, carry_ref):
    i = pl.program_id(0)
    tm, d = x_ref.shape

    ga = gate_ref[:, :d].astype(F32)
    gb = gate_ref[:, d:].astype(F32)
    merged = ga * _dot(oa_ref[...], woa_ref[...]) + gb * _dot(ob_ref[...], wob_ref[...])
    x1 = x_ref[...] + _dot(merged.astype(BF16), wout_ref[...])
    x1_ref[...] = x1
    h2 = x1 * lax.rsqrt(jnp.mean(x1 * x1, axis=-1, keepdims=True) + EPS) * g2_ref[...]
    _store_row_tiles(h2_ref, h2)

    h_hi = h2.astype(BF16)
    h_lo = (h2 - h_hi.astype(F32)).astype(BF16)
    logits = _dot(h_hi, wrh_ref[...]) + _dot(h_lo, wrh_ref[...]) + _dot(h_hi, wrl_ref[...]) + br_ref[...]
    lane = _lane_iota((tm, LANES))
    lanef = lane.astype(F32)
    logits = jnp.where(lane < N_EXPERTS, logits, NEG)

    sels, vals = [], []
    for _ in range(TOP_K):
        mx = jnp.max(logits, axis=-1, keepdims=True)
        ix = jnp.min(jnp.where(logits == mx, lanef, float(LANES)), axis=-1, keepdims=True)
        sel = lanef == ix
        sels.append((sel, ix))
        vals.append(mx)
        logits = jnp.where(sel, 3.0 * NEG, logits)
    exps = [jnp.exp(v - vals[0]) for v in vals]
    den = exps[0] + exps[1] + exps[2] + exps[3]

    @pl.when(i == 0)
    def _():
        carry_ref[...] = jnp.zeros_like(carry_ref)

    onehot = jnp.zeros((tm, LANES), F32)
    for sel, _ in sels:
        onehot = onehot + sel.astype(F32)
    row = lax.broadcasted_iota(jnp.int32, (tm, tm), 0)
    col = lax.broadcasted_iota(jnp.int32, (tm, tm), 1)
    before = _dot((col < row).astype(BF16), onehot.astype(BF16)) + carry_ref[0:1, :]
    total = carry_ref[0:1, :] + jnp.sum(onehot, axis=0, keepdims=True)
    carry_ref[0:1, :] = total
    cnt_ref[...] = jnp.broadcast_to(total, cnt_ref.shape)

    idx_out = jnp.zeros((tm, LANES), F32)
    gw_out = jnp.zeros((tm, LANES), F32)
    rank_out = jnp.zeros((tm, LANES), F32)
    for k, (sel, ix) in enumerate(sels):
        slot = lane == k
        rank = jnp.sum(jnp.where(sel, before, 0.0), axis=-1, keepdims=True)
        idx_out = jnp.where(slot, ix, idx_out)
        gw_out = jnp.where(slot, exps[k] / den, gw_out)
        rank_out = jnp.where(slot, rank, rank_out)
    idx_ref[...] = idx_out.astype(jnp.int32)
    gw_ref[...] = gw_out
    rank_ref[...] = rank_out.astype(jnp.int32)


def _merge(out_a, out_b, gates, x2, w_o_swa, w_o_fox, w_out, norm2_g, w_router, b_router):
    n, d = x2.shape
    tm = MERGE_TM
    wr = jnp.pad(w_router, ((0, 0), (0, LANES - N_EXPERTS)))
    wrh = wr.astype(BF16)
    wrl = (wr - wrh.astype(F32)).astype(BF16)
    br = jnp.pad(b_router, (0, LANES - N_EXPERTS)).reshape(1, LANES)
    consts = (w_o_swa.astype(BF16), w_o_fox.astype(BF16), w_out.astype(BF16), norm2_g.reshape(1, d), wrh, wrl, br)
    row = lambda w: pl.BlockSpec((tm, w), lambda i: (i, 0))
    full = lambda a: pl.BlockSpec(a.shape, lambda i: (0,) * a.ndim)
    return pl.pallas_call(
        _merge_kernel,
        grid=(n // tm,),
        in_specs=[row(out_a.shape[1]), row(out_b.shape[1]), row(gates.shape[1]), row(d)] + [full(a) for a in consts],
        out_specs=[row(d), pl.BlockSpec((tm * ROW_SUB, LANES), lambda i: (i, 0)), row(LANES), row(LANES), row(LANES),
                   pl.BlockSpec((8, LANES), lambda i: (0, 0))],
        out_shape=[
            jax.ShapeDtypeStruct((n, d), F32),
            jax.ShapeDtypeStruct((n * ROW_SUB, LANES), F32),
            jax.ShapeDtypeStruct((n, LANES), jnp.int32),
            jax.ShapeDtypeStruct((n, LANES), F32),
            jax.ShapeDtypeStruct((n, LANES), jnp.int32),
            jax.ShapeDtypeStruct((8, LANES), F32),
        ],
        scratch_shapes=[pltpu.VMEM((8, LANES), F32)],
        compiler_params=_params(("arbitrary",)),
        name="merge",
    )(out_a, out_b, gates, x2, *consts)


def _idx_copy(dest_ref, idx_ref, sem_ref, tile, slot, per_tile):
    src = dest_ref.at[pl.ds(pl.multiple_of(tile * per_tile, per_tile), per_tile)]
    dst = idx_ref.at[pl.ds(pl.multiple_of(slot * per_tile, per_tile), per_tile)]
    return pltpu.make_async_copy(src, dst, sem_ref.at[slot])


def _dispatch_kernel(tail_start_ref, tail_flag_ref, n_used_ref, dest_ref, h_ref, xin_ref, idx_ref, zeros_ref, idx_sem,
                     row_sem, zero_sem):
    i = pl.program_id(0)
    n_tiles = pl.num_programs(0)
    t = h_ref.shape[0] // ROW_SUB
    per_tile = t * TOP_K
    slot = i % 2
    tm = zeros_ref.shape[0] // ROW_SUB
    n_blocks = xin_ref.shape[0] // zeros_ref.shape[0]

    def zero_copy(start):
        start = pl.multiple_of(start * ROW_SUB, ROW_SUB)
        return pltpu.make_async_copy(zeros_ref, xin_ref.at[pl.ds(start, tm * ROW_SUB)], zero_sem)

    def unused_block(op):
        def body(blk, carry):
            op(zero_copy(blk * tm))
            return carry
        lax.fori_loop(n_used_ref[0], n_blocks, body, 0)

    @pl.when(i == 0)
    def _():
        zeros_ref[...] = jnp.zeros_like(zeros_ref)
        _idx_copy(dest_ref, idx_ref, idx_sem, 0, 0, per_tile).start()
        for e in range(N_EXPERTS):
            @pl.when(tail_flag_ref[e] > 0)
            def _():
                zero_copy(tail_start_ref[e]).start()
        unused_block(lambda c: c.start())
        for e in range(N_EXPERTS):
            @pl.when(tail_flag_ref[e] > 0)
            def _():
                zero_copy(tail_start_ref[e]).wait()
        unused_block(lambda c: c.wait())

    @pl.when(i + 1 < n_tiles)
    def _():
        _idx_copy(dest_ref, idx_ref, idx_sem, i + 1, 1 - slot, per_tile).start()

    _idx_copy(dest_ref, idx_ref, idx_sem, i, slot, per_tile).wait()

    base = slot * per_tile

    def row_copy(r, k):
        return pltpu.make_async_copy(_row_tile(h_ref, r), _row_tile(xin_ref, idx_ref[base + r * TOP_K + k]), row_sem)

    def issue(r, carry):
        for k in range(TOP_K):
            row_copy(r, k).start(priority=k % 2)
        return carry

    def drain(r, carry):
        for k in range(TOP_K):
            row_copy(r, k).wait()
        return carry

    lax.fori_loop(0, t, issue, 0, unroll=8)
    lax.fori_loop(0, t, drain, 0, unroll=8)


def _dispatch(dest, h2t, tail_start, tail_flag, n_used, cap):
    n = h2t.shape[0] // ROW_SUB
    t = DISPATCH_T
    return pl.pallas_call(
        _dispatch_kernel,
        grid_spec=pltpu.PrefetchScalarGridSpec(
            num_scalar_prefetch=3,
            grid=(n // t,),
            in_specs=[pl.BlockSpec(memory_space=pl.ANY), pl.BlockSpec((t * ROW_SUB, LANES), lambda i, *_: (i, 0))],
            out_specs=pl.BlockSpec(memory_space=pl.ANY),
            scratch_shapes=[
                pltpu.SMEM((2 * t * TOP_K,), jnp.int32),
                pltpu.VMEM((EXPERT_TM * ROW_SUB, LANES), F32),
                pltpu.SemaphoreType.DMA((2,)),
                pltpu.SemaphoreType.DMA,
                pltpu.SemaphoreType.DMA,
            ],
        ),
        out_shape=jax.ShapeDtypeStruct((cap * ROW_SUB, LANES), F32),
        compiler_params=_params(("arbitrary",)),
        name="dispatch",
    )(tail_start, tail_flag, n_used, dest, h2t)


def _expert_kernel(blk_expert_ref, n_used_ref, x_ref, wgu_ref, bgu_ref, wd_ref, bd_ref, y_ref, wgu_bf, wd_bf):
    i = pl.program_id(0)
    f = wd_ref.shape[1]
    used = i < n_used_ref[0]

    new_expert = (i == 0) | (blk_expert_ref[i] != blk_expert_ref[jnp.maximum(i - 1, 0)])

    @pl.when(used & new_expert)
    def _():
        step = EXPERT_FC
        for r in range(0, wgu_ref.shape[1], step):
            wgu_bf[r:r + step, :] = wgu_ref[0, r:r + step, :].astype(BF16)
        for r in range(0, f, step):
            wd_bf[r:r + step, :] = wd_ref[0, r:r + step, :].astype(BF16)

    @pl.when(used)
    def _():
        x = _load_row_tiles(x_ref).astype(BF16)
        y = bd_ref[0]
        for c in range(f // EXPERT_FC):
            lo, hi = c * EXPERT_FC, (c + 1) * EXPERT_FC
            g = _dot(x, wgu_bf[:, lo:hi]) + bgu_ref[0, :, lo:hi]
            u = _dot(x, wgu_bf[:, f + lo:f + hi]) + bgu_ref[0, :, f + lo:f + hi]
            g = jnp.minimum(g, SWIGLU_LIMIT)
            u = jnp.clip(u, -SWIGLU_LIMIT, SWIGLU_LIMIT)
            a = (u + 1.0) * (g * (1.0 / (1.0 + jnp.exp(-SWIGLU_ALPHA * g))))
            y = y + _dot(a.astype(BF16), wd_bf[lo:hi, :])
        _store_row_tiles(y_ref, y)

    @pl.when(i >= n_used_ref[0])
    def _():
        y_ref[...] = jnp.zeros_like(y_ref)


def _experts(xin, blk_expert, n_used, w_gate_up, b_gate_up, w_down, b_down):
    cap = xin.shape[0] // ROW_SUB
    tm = EXPERT_TM
    ne, d, f2 = w_gate_up.shape
    f = f2 // 2
    rows = lambda i, be, nu: (jnp.minimum(i, nu[0] - 1), 0)
    by_expert = lambda i, be, nu: (be[i], 0, 0)
    return pl.pallas_call(
        _expert_kernel,
        grid_spec=pltpu.PrefetchScalarGridSpec(
            num_scalar_prefetch=2,
            grid=(cap // tm,),
            in_specs=[
                pl.BlockSpec((tm * ROW_SUB, LANES), rows),
                pl.BlockSpec((1, d, f2), by_expert),
                pl.BlockSpec((1, 1, f2), by_expert),
                pl.BlockSpec((1, f, d), by_expert),
                pl.BlockSpec((1, 1, d), by_expert),
            ],
            out_specs=pl.BlockSpec((tm * ROW_SUB, LANES), lambda i, be, nu: (i, 0)),
            scratch_shapes=[pltpu.VMEM((d, f2), BF16), pltpu.VMEM((f, d), BF16)],
        ),
        out_shape=jax.ShapeDtypeStruct((cap * ROW_SUB, LANES), F32),
        compiler_params=_params(("arbitrary",)),
        name="experts",
    )(blk_expert, n_used, xin, w_gate_up, b_gate_up.reshape(ne, 1, f2), w_down, b_down.reshape(ne, 1, d))


def _combine_kernel(dest_ref, y_ref, x1_ref, gw_ref, o_ref, idx_ref, rows_ref, idx_sem, row_sem):
    i = pl.program_id(0)
    n_tiles = pl.num_programs(0)
    t = x1_ref.shape[0]
    per_tile = t * TOP_K
    slot = i % 2

    def row_copy(tile_slot, r, k, d):
        dst = _row_tile(rows_ref, (tile_slot * TOP_K + k) * t + r)
        return pltpu.make_async_copy(_row_tile(y_ref, d), dst, row_sem.at[tile_slot])

    def gather(tile_slot):
        base = tile_slot * per_tile

        def issue(r, carry):
            for k in range(TOP_K):
                row_copy(tile_slot, r, k, idx_ref[base + r * TOP_K + k]).start(priority=k % 2)
            return carry

        lax.fori_loop(0, t, issue, 0, unroll=8)

    @pl.when(i == 0)
    def _():
        _idx_copy(dest_ref, idx_ref, idx_sem, 0, 0, per_tile).start()
        _idx_copy(dest_ref, idx_ref, idx_sem, 0, 0, per_tile).wait()
        gather(0)

        @pl.when(n_tiles > 1)
        def _():
            _idx_copy(dest_ref, idx_ref, idx_sem, 1, 1, per_tile).start()

    @pl.when(i + 1 < n_tiles)
    def _():
        _idx_copy(dest_ref, idx_ref, idx_sem, i + 1, 1 - slot, per_tile).wait()
        gather(1 - slot)

    @pl.when(i + 2 < n_tiles)
    def _():
        _idx_copy(dest_ref, idx_ref, idx_sem, i + 2, slot, per_tile).start()

    def drain(r, carry):
        for k in range(TOP_K):
            row_copy(slot, r, k, 0).wait()
        return carry

    lax.fori_loop(0, t, drain, 0, unroll=8)

    weights = [jnp.broadcast_to(gw_ref[:, k:k + 1], (t, LANES)) for k in range(TOP_K)]
    for j in range(ROW_SUB):
        acc = x1_ref[:, j * LANES:(j + 1) * LANES]
        for k in range(TOP_K):
            acc = acc + weights[k] * rows_ref[_group(rows_ref, j, t, base=(slot * TOP_K + k) * t)]
        o_ref[:, j * LANES:(j + 1) * LANES] = acc


def _combine(dest, yb, x1, gw):
    n, d = x1.shape
    t = COMBINE_T
    return pl.pallas_call(
        _combine_kernel,
        grid=(n // t,),
        in_specs=[
            pl.BlockSpec(memory_space=pl.ANY),
            pl.BlockSpec(memory_space=pl.ANY),
            pl.BlockSpec((t, d), lambda i: (i, 0)),
            pl.BlockSpec((t, LANES), lambda i: (i, 0)),
        ],
        out_specs=pl.BlockSpec((t, d), lambda i: (i, 0)),
        out_shape=jax.ShapeDtypeStruct((n, d), F32),
        scratch_shapes=[
            pltpu.SMEM((2 * t * TOP_K,), jnp.int32),
            pltpu.VMEM((2 * TOP_K * t * ROW_SUB, LANES), F32),
            pltpu.SemaphoreType.DMA((2,)),
            pltpu.SemaphoreType.DMA((2,)),
        ],
        compiler_params=_params(("arbitrary",)),
        name="combine",
    )(dest, yb, x1, gw)


def _layer(x, positions, norm1_g, w_in, b_forget, b_gate, q_norm_swa, k_norm_swa, sinks, q_norm_fox, k_norm_fox,
           w_o_swa, w_o_fox, w_out, norm2_g, w_router, b_router, w_gate_up, b_gate_up, w_down, b_down):
    b, s, d = x.shape
    n = b * s
    assert s % max(PROJ_TM, SWA_TQ, FOX_TQ) == 0 and n % max(MERGE_TM, DISPATCH_T, COMBINE_T) == 0
    assert d == ROW_SUB * LANES
    x2 = x.reshape(n, d)

    qa, ka2, va2, qf, kf, vf, gates = _in_proj(x2, positions.reshape(n, 1), norm1_g, w_in, b_forget, b_gate,
                                               q_norm_swa, k_norm_swa, q_norm_fox, k_norm_fox, s)
    seq = lambda a: a.reshape(b, s, a.shape[1])
    out_a = _swa(seq(qa), seq(ka2), seq(va2), sinks).reshape(n, SWA_Q_W)
    out_b = _fox(seq(qf), seq(kf), seq(vf)).reshape(n, FOX_W)

    x1, h2t, idx, gw, rank, totals = _merge(out_a, out_b, gates, x2, w_o_swa, w_o_fox, w_out, norm2_g, w_router,
                                            b_router)

    tm = EXPERT_TM
    cap = n * TOP_K + N_EXPERTS * tm
    counts = totals[0, :N_EXPERTS].astype(jnp.int32)
    padded = (counts + tm - 1) // tm * tm
    pad_end = jnp.cumsum(padded)
    pad_start = pad_end - padded
    dest = (pad_start[idx[:, :TOP_K]] + rank[:, :TOP_K]).reshape(n * TOP_K)
    blk_start = jnp.arange(cap // tm, dtype=jnp.int32) * tm
    blk_expert = jnp.minimum(jnp.sum(pad_end[None, :] <= blk_start[:, None], axis=1), N_EXPERTS - 1).astype(jnp.int32)
    n_used = (pad_end[-1:] // tm).astype(jnp.int32)

    xin = _dispatch(dest, h2t, (pad_end - tm).astype(jnp.int32), (padded > 0).astype(jnp.int32), n_used, cap)
    yb = _experts(xin, blk_expert, n_used, w_gate_up, b_gate_up, w_down, b_down)
    return _combine(dest, yb, x1, gw).reshape(b, s, d)


def kernel(x, positions, norm1_g, w_in, b_forget, b_gate, q_norm_swa, k_norm_swa, sinks, q_norm_fox, k_norm_fox,
           w_o_swa, w_o_fox, w_out, norm2_g, w_router, b_router, w_gate_up, b_gate_up, w_down, b_down):
    for l in range(norm1_g.shape[0]):
        x = _layer(x, positions, norm1_g[l], w_in[l], b_forget[l], b_gate[l], q_norm_swa[l], k_norm_swa[l],
                   sinks[l], q_norm_fox[l], k_norm_fox[l], w_o_swa[l], w_o_fox[l], w_out[l], norm2_g[l],
                   w_router[l], b_router[l], w_gate_up[l], b_gate_up[l], w_down[l], b_down[l])
    return x
```

```python
import functools

import jax
import jax.numpy as jnp
import numpy as np
from jax import lax
from jax.experimental import pallas as pl
from jax.experimental.pallas import tpu as pltpu

HEAD_DIM = 64
SWA_Q_HEADS = 8
SWA_KV_HEADS = 2
SWA_BLOCK = 128
FOX_HEADS = 8
ROPE_THETA = 10000.0
N_EXPERTS = 32
TOP_K = 4
SWIGLU_LIMIT = 7.0
SWIGLU_ALPHA = 1.702
EPS = 1e-6
NEG = -1e30
ATTN_SCALE = HEAD_DIM ** -0.5
LOG2E = 1.4426950408889634

LANES = 128
VMEM_LIMIT_BYTES = 56 * 1024 * 1024

SWA_Q_W = SWA_Q_HEADS * HEAD_DIM
SWA_KV_W = SWA_KV_HEADS * HEAD_DIM
FOX_W = FOX_HEADS * HEAD_DIM
FOX_AUG_W = FOX_HEADS * LANES

PROJ_TM = 512
SWA_TQ = 512
FOX_TQ = 1024
FOX_SUB = 512
FOX_TK = 512
MERGE_TM = 512
DISPATCH_T = 512
EXPERT_TM = 512
EXPERT_FC = 256
COMBINE_T = 512

BF16 = jnp.bfloat16
F32 = jnp.float32


def _params(sem):
    return pltpu.CompilerParams(dimension_semantics=sem, vmem_limit_bytes=VMEM_LIMIT_BYTES)


def _dot(a, b):
    return jnp.dot(a, b, preferred_element_type=F32)


def _dot_nt(a, b):
    return lax.dot_general(a, b, (((1,), (1,)), ((), ())), preferred_element_type=F32)


def _split3(v):
    hi = v.astype(BF16)
    r1 = v - hi.astype(F32)
    mid = r1.astype(BF16)
    lo = (r1 - mid.astype(F32)).astype(BF16)
    return hi, mid, lo


def _pack3(v):
    hi, mid, lo = _split3(v)
    packed = hi.astype(F32) + pltpu.roll(mid.astype(F32), FOX_HEADS, 1) + pltpu.roll(lo.astype(F32), 2 * FOX_HEADS, 1)
    return packed.astype(BF16)


def _lane_iota(shape):
    return lax.broadcasted_iota(jnp.int32, shape, len(shape) - 1)


def _head_norm(t, gain, bd):
    ss = _dot((t * t).astype(BF16), bd)
    return t * lax.rsqrt(ss * (1.0 / HEAD_DIM) + EPS) * gain


def _rope(t, cos, sin_signed, first_half):
    partner = jnp.where(first_half, pltpu.roll(t, LANES - HEAD_DIM // 2, 1), pltpu.roll(t, HEAD_DIM // 2, 1))
    return t * cos + partner * sin_signed


def _in_proj_kernel(tiles_per_batch, x_ref, pos_ref, g1_ref, wqkv_ref, wfl_ref, wgl_ref, bfl_ref, bgl_ref,
                    gqa_ref, gka_ref, gqf_ref, gkf_ref, invf_ref, eq_ref, ek_ref, oneq_ref, onek_ref,
                    qa_ref, ka_ref, va_ref, qf_ref, kf_ref, vf_ref, gate_ref, carry_ref):
    i = pl.program_id(0)
    tm = x_ref.shape[0]

    x = x_ref[...]
    h = x * lax.rsqrt(jnp.mean(x * x, axis=-1, keepdims=True) + EPS) * g1_ref[...]
    h = h.astype(BF16)

    lane = _lane_iota((tm, LANES))
    in_low_head = lane < HEAD_DIM
    first_half = (lane & (HEAD_DIM - 1)) < (HEAD_DIM // 2)
    shift = HEAD_DIM.bit_length() - 1
    rr = lax.broadcasted_iota(jnp.int32, (2 * LANES, 2 * LANES), 0) >> shift
    cc = lax.broadcasted_iota(jnp.int32, (2 * LANES, 2 * LANES), 1) >> shift
    bd2 = (rr == cc).astype(BF16)
    bd = bd2[:LANES, :LANES]

    ang = pos_ref[...].astype(F32) * invf_ref[...]
    cos = jnp.cos(ang)
    sin_signed = jnp.where(first_half, -jnp.sin(ang), jnp.sin(ang))

    fl = _dot(h, wfl_ref[...]) + bfl_ref[...]
    logf = jnp.minimum(fl, 0.0) - jnp.log(1.0 + jnp.exp(-jnp.abs(fl)))
    logf = jnp.where(lane < FOX_HEADS, logf, 0.0)
    row = lax.broadcasted_iota(jnp.int32, (tm, tm), 0)
    col = lax.broadcasted_iota(jnp.int32, (tm, tm), 1)
    tri = (col <= row).astype(BF16)
    c3 = _dot(tri, _pack3(logf))
    csum = c3 + pltpu.roll(c3, LANES - FOX_HEADS, 1) + pltpu.roll(c3, LANES - 2 * FOX_HEADS, 1)
    csum = jnp.where(lane < FOX_HEADS, csum, 0.0)

    @pl.when(i % tiles_per_batch == 0)
    def _():
        carry_ref[...] = jnp.zeros_like(carry_ref)

    csum = csum + carry_ref[0:1, :]
    carry_ref[0:1, :] = csum[tm - 1:tm, :]
    c_parts = _pack3(csum * LOG2E)

    wide = 2 * LANES
    for c in range(SWA_Q_W // wide):
        t = _dot(h, wqkv_ref[:, c * wide:(c + 1) * wide])
        t = _head_norm(t, gqa_ref[:, c * wide:(c + 1) * wide], bd2)
        for g in range(2):
            lo = c * wide + g * LANES
            qa_ref[:, lo:lo + LANES] = _rope(t[:, g * LANES:(g + 1) * LANES], cos, sin_signed, first_half).astype(BF16)
    off = SWA_Q_W
    t = _dot(h, wqkv_ref[:, off:off + LANES])
    t = _rope(_head_norm(t, gka_ref[...], bd), cos, sin_signed, first_half)
    ka_ref[:, 0:LANES] = t.astype(BF16)
    ka_ref[:, LANES:2 * LANES] = pltpu.roll(t, HEAD_DIM, 1).astype(BF16)
    off += SWA_KV_W
    t = _dot(h, wqkv_ref[:, off:off + LANES])
    va_ref[:, 0:LANES] = t.astype(BF16)
    va_ref[:, LANES:2 * LANES] = pltpu.roll(t, HEAD_DIM, 1).astype(BF16)
    off += SWA_KV_W

    aug_q = _dot(c_parts, eq_ref[...]) + oneq_ref[...]
    aug_k = _dot(c_parts, ek_ref[...]) + onek_ref[...]
    for name_off, gain_ref, aug, out_ref in ((off, gqf_ref, aug_q, qf_ref), (off + FOX_W, gkf_ref, aug_k, kf_ref)):
        for c in range(FOX_W // wide):
            t2 = _dot(h, wqkv_ref[:, name_off + c * wide:name_off + (c + 1) * wide])
            t2 = _head_norm(t2, gain_ref[:, c * wide:(c + 1) * wide], bd2)
            for g in range(2):
                t = t2[:, g * LANES:(g + 1) * LANES]
                even = 2 * (2 * c + g) * LANES
                odd = even + LANES
                out_ref[:, even:even + LANES] = jnp.where(in_low_head, t, aug[:, even:even + LANES]).astype(BF16)
                out_ref[:, odd:odd + LANES] = jnp.where(in_low_head, aug[:, odd:odd + LANES], t).astype(BF16)
    off += 2 * FOX_W
    for c in range(FOX_W // LANES):
        t = _dot(h, wqkv_ref[:, off + c * LANES:off + (c + 1) * LANES])
        vf_ref[:, 2 * c * LANES:(2 * c + 1) * LANES] = jnp.where(in_low_head, t, 1.0).astype(BF16)
        vf_ref[:, (2 * c + 1) * LANES:(2 * c + 2) * LANES] = jnp.where(in_low_head, 1.0, t).astype(BF16)

    gw = 512
    for c in range(gate_ref.shape[1] // gw):
        z = _dot(h, wgl_ref[:, c * gw:(c + 1) * gw]) + bgl_ref[:, c * gw:(c + 1) * gw]
        gate_ref[:, c * gw:(c + 1) * gw] = (1.0 / (1.0 + jnp.exp(-z))).astype(BF16)


def _aug_placement():
    eq = np.zeros((LANES, FOX_AUG_W), np.float32)
    ek = np.zeros((LANES, FOX_AUG_W), np.float32)
    oneq = np.zeros((1, FOX_AUG_W), np.float32)
    onek = np.zeros((1, FOX_AUG_W), np.float32)
    for h in range(FOX_HEADS):
        base = h * LANES + (HEAD_DIM if h % 2 == 0 else 0)
        for p in range(3):
            eq[p * FOX_HEADS + h, base + p] = 1.0
            ek[p * FOX_HEADS + h, base + 3 + p] = -1.0
            oneq[0, base + 3 + p] = 1.0
            onek[0, base + p] = 1.0
    return jnp.asarray(eq, BF16), jnp.asarray(ek, BF16), jnp.asarray(oneq), jnp.asarray(onek)


def _in_proj(x2, pos2, norm1_g, w_in, b_forget, b_gate, q_norm_swa, k_norm_swa, q_norm_fox, k_norm_fox, seq):
    n, d = x2.shape
    tm = PROJ_TM
    n_qkv = SWA_Q_W + 2 * SWA_KV_W + 3 * FOX_W
    wqkv = w_in[:, :n_qkv].astype(BF16)
    wfl = jnp.pad(w_in[:, n_qkv:n_qkv + FOX_HEADS], ((0, 0), (0, LANES - FOX_HEADS))).astype(BF16)
    wgl = w_in[:, n_qkv + FOX_HEADS:].astype(BF16)
    n_gate = wgl.shape[1]
    bfl = jnp.pad(b_forget, (0, LANES - FOX_HEADS)).reshape(1, LANES)
    bgl = b_gate.reshape(1, n_gate)
    gqa = (jnp.tile(q_norm_swa, SWA_Q_HEADS) * ATTN_SCALE).reshape(1, SWA_Q_W)
    gka = jnp.tile(k_norm_swa, SWA_KV_HEADS).reshape(1, SWA_KV_W)
    gqf = (jnp.tile(q_norm_fox, FOX_HEADS) * (ATTN_SCALE * LOG2E)).reshape(1, FOX_W)
    gkf = jnp.tile(k_norm_fox, FOX_HEADS).reshape(1, FOX_W)
    half = HEAD_DIM // 2
    inv = ROPE_THETA ** (-jnp.arange(half, dtype=F32) / half)
    invf = jnp.tile(inv, LANES // half).reshape(1, LANES)
    eq, ek, oneq, onek = _aug_placement()

    row = lambda w: pl.BlockSpec((tm, w), lambda i: (i, 0))
    full = lambda a: pl.BlockSpec(a.shape, lambda i: (0,) * a.ndim)
    consts = (norm1_g.reshape(1, d), wqkv, wfl, wgl, bfl, bgl, gqa, gka, gqf, gkf, invf, eq, ek, oneq, onek)
    out_widths = (SWA_Q_W, 2 * SWA_KV_W, 2 * SWA_KV_W, FOX_AUG_W, FOX_AUG_W, 2 * FOX_W, n_gate)
    return pl.pallas_call(
        functools.partial(_in_proj_kernel, seq // tm),
        grid=(n // tm,),
        in_specs=[row(d), row(1)] + [full(a) for a in consts],
        out_specs=[row(w) for w in out_widths],
        out_shape=[jax.ShapeDtypeStruct((n, w), BF16) for w in out_widths],
        scratch_shapes=[pltpu.VMEM((8, LANES), F32)],
        compiler_params=_params(("arbitrary",)),
        name="in_proj",
    )(x2, pos2, *consts)


def _swa_kernel(sinks_ref, q_ref, kc_ref, kp_ref, vc_ref, vp_ref, o_ref):
    i = pl.program_id(1)
    tq = q_ref.shape[1]
    blk = SWA_BLOCK
    lane = _lane_iota((2 * blk, LANES))
    low = lane < HEAD_DIM
    qi = lax.broadcasted_iota(jnp.int32, (blk, 2 * blk), 0)
    kj = lax.broadcasted_iota(jnp.int32, (blk, 2 * blk), 1)
    cur_ok = (kj >= blk) & (kj - blk <= qi)
    prev_ok = (kj < blk) & (kj > qi)
    zero = jnp.zeros((2 * blk, LANES), BF16)
    group = SWA_Q_HEADS // SWA_KV_HEADS

    for j in range(tq // blk):
        if j == 0:
            k_prev, v_prev = kp_ref[0], vp_ref[0]
            valid = cur_ok | (prev_ok & (i > 0))
        else:
            k_prev, v_prev = kc_ref[0, (j - 1) * blk:j * blk, :], vc_ref[0, (j - 1) * blk:j * blk, :]
            valid = cur_ok | prev_ok
        kwin = jnp.concatenate([k_prev, kc_ref[0, j * blk:(j + 1) * blk, :]], axis=0)
        vwin = jnp.concatenate([v_prev, vc_ref[0, j * blk:(j + 1) * blk, :]], axis=0)
        nat_k, swp_k = kwin[:, :LANES], kwin[:, LANES:]
        nat_v, swp_v = vwin[:, :LANES], vwin[:, LANES:]
        k_var = ((jnp.where(low, nat_k, zero), jnp.where(low, zero, swp_k)),
                 (jnp.where(low, swp_k, zero), jnp.where(low, zero, nat_k)))
        v_var = ((jnp.where(low, nat_v, zero), jnp.where(low, zero, swp_v)),
                 (jnp.where(low, swp_v, zero), jnp.where(low, zero, nat_v)))
        for c in range(SWA_Q_W // LANES):
            qc = q_ref[0, j * blk:(j + 1) * blk, c * LANES:(c + 1) * LANES]
            out = jnp.zeros((blk, LANES), F32)
            for half in range(2):
                head = 2 * c + half
                kv = head // group
                s = jnp.where(valid, _dot_nt(qc, k_var[kv][half]), NEG)
                sink = sinks_ref[head]
                m = jnp.maximum(jnp.max(s, axis=-1, keepdims=True), sink)
                p = jnp.exp(s - m)
                den = jnp.sum(p, axis=-1, keepdims=True) + jnp.exp(sink - m)
                out = out + _dot(p.astype(BF16), v_var[kv][half]) / den
            o_ref[0, j * blk:(j + 1) * blk, c * LANES:(c + 1) * LANES] = out.astype(BF16)


def _swa(qa, ka2, va2, sinks):
    b, s, _ = qa.shape
    tq = SWA_TQ
    per = tq // SWA_BLOCK
    cur = lambda w: pl.BlockSpec((1, tq, w), lambda bi, i, sk: (bi, i, 0))
    prev = lambda w: pl.BlockSpec((1, SWA_BLOCK, w), lambda bi, i, sk: (bi, jnp.maximum(i * per - 1, 0), 0))
    return pl.pallas_call(
        _swa_kernel,
        grid_spec=pltpu.PrefetchScalarGridSpec(
            num_scalar_prefetch=1,
            grid=(b, s // tq),
            in_specs=[cur(SWA_Q_W), cur(2 * SWA_KV_W), prev(2 * SWA_KV_W), cur(2 * SWA_KV_W), prev(2 * SWA_KV_W)],
            out_specs=cur(SWA_Q_W),
        ),
        out_shape=jax.ShapeDtypeStruct((b, s, SWA_Q_W), BF16),
        compiler_params=_params(("arbitrary", "arbitrary")),
        name="swa",
    )(sinks, qa, ka2, ka2, va2, va2)


def _fox_kernel(q_ref, k_ref, v_ref, o_ref, m_ref, acc_ref, s_ref):
    qi = pl.program_id(2)
    sub, tk = FOX_SUB, FOX_TK
    tq = q_ref.shape[1]
    n_sub = tq // sub
    n_chunk = tk // LANES
    row = lax.broadcasted_iota(jnp.int32, (sub, LANES), 0)
    col = lax.broadcasted_iota(jnp.int32, (sub, LANES), 1)
    chains = [(half, sb) for sb in range(n_sub) for half in range(2)]

    m_ref[...] = jnp.full_like(m_ref, NEG)
    acc_ref[...] = jnp.zeros_like(acc_ref)

    def scores(buf, chain, kb):
        half, sb = chains[chain]
        start = pl.multiple_of(kb * tk, tk)
        q = q_ref[0, sb * sub:(sb + 1) * sub, half * LANES:(half + 1) * LANES]
        k = k_ref[0, pl.ds(start, tk), half * LANES:(half + 1) * LANES]
        s_ref[buf, chain] = _dot_nt(q, k)

    def fold(buf, chain, kb, mask_offset):
        half, sb = chains[chain]
        start = pl.multiple_of(kb * tk, tk)
        v = v_ref[0, pl.ds(start, tk), half * LANES:(half + 1) * LANES]
        s = [s_ref[buf, chain, :, c * LANES:(c + 1) * LANES] for c in range(n_chunk)]
        if mask_offset is not None:
            s = [jnp.where(col + (c * LANES + mask_offset) <= row, s[c], NEG) for c in range(n_chunk)]
        m_prev = m_ref[chain]
        peak = functools.reduce(jnp.maximum, s)
        m_new = jnp.maximum(m_prev, jnp.max(peak, axis=-1, keepdims=True))
        alpha = jnp.exp2(m_prev - m_new)
        p = [jnp.exp2(sc - m_new) for sc in s]
        pv = _dot(jnp.concatenate(p, axis=1).astype(BF16), v)
        acc_ref[chain] = alpha * acc_ref[chain] + pv
        m_ref[chain] = m_new

    assert tq // tk == 2
    every = range(len(chains))
    for chain in every:
        scores(0, chain, 0)

    def body(t, carry):
        for chain in every:
            scores(1, chain, 2 * t + 1)
        for chain in every:
            fold(0, chain, 2 * t, None)
        for chain in every:
            scores(0, chain, 2 * t + 2)
        for chain in every:
            fold(1, chain, 2 * t + 1, None)
        return carry

    lax.fori_loop(0, qi, body, 0)

    def visibility(j, sb):
        first_key, first_row = j * tk, sb * sub
        if first_key + tk - 1 <= first_row:
            return True, None
        if first_key <= first_row + sub - 1:
            return True, first_key - first_row
        return False, None

    for chain, (half, sb) in enumerate(chains):
        if visibility(1, sb)[0]:
            scores(1, chain, 2 * qi + 1)
    for j in range(2):
        for chain, (half, sb) in enumerate(chains):
            visible, mask_offset = visibility(j, sb)
            if visible:
                fold(j, chain, 2 * qi + j, mask_offset)

    low = _lane_iota((sub, LANES)) < HEAD_DIM
    for sb in range(n_sub):
        even, odd = acc_ref[2 * sb], acc_ref[2 * sb + 1]
        even = even / pltpu.roll(even, HEAD_DIM, 1)
        odd = odd / pltpu.roll(odd, HEAD_DIM, 1)
        o_ref[0, sb * sub:(sb + 1) * sub, :] = jnp.where(low, even, odd).astype(BF16)


def _fox(qf, kf, vf):
    b, s, _ = qf.shape
    tq = FOX_TQ
    pairs = FOX_HEADS // 2
    n_chain = 2 * (tq // FOX_SUB)
    return pl.pallas_call(
        _fox_kernel,
        grid=(b, pairs, s // tq),
        in_specs=[
            pl.BlockSpec((1, tq, 2 * LANES), lambda bi, p, i: (bi, i, p)),
            pl.BlockSpec((1, s, 2 * LANES), lambda bi, p, i: (bi, 0, p)),
            pl.BlockSpec((1, s, 2 * LANES), lambda bi, p, i: (bi, 0, p)),
        ],
        out_specs=pl.BlockSpec((1, tq, LANES), lambda bi, p, i: (bi, i, p)),
        out_shape=jax.ShapeDtypeStruct((b, s, FOX_W), BF16),
        scratch_shapes=[pltpu.VMEM((n_chain, FOX_SUB, LANES), F32)] * 2
        + [pltpu.VMEM((2, n_chain, FOX_SUB, FOX_TK), F32)],
        compiler_params=_params(("arbitrary", "arbitrary", "arbitrary")),
        name="fox",
    )(qf, kf, vf)


ROW_SUB = 8


def _group(ref, j, rows, base=0):
    return (pl.ds(base * ROW_SUB + j, rows, stride=ROW_SUB), slice(None))


def _store_row_tiles(ref, v):
    for j in range(ROW_SUB):
        ref[_group(ref, j, v.shape[0])] = v[:, j * LANES:(j + 1) * LANES]


def _load_row_tiles(ref):
    rows = ref.shape[0] // ROW_SUB
    return jnp.concatenate([ref[_group(ref, j, rows)] for j in range(ROW_SUB)], axis=1)


def _tile_at(ref, first_sublane):
    return ref.at[pl.ds(pl.multiple_of(first_sublane, ROW_SUB), ROW_SUB)]


def _row_tile(ref, r):
    return _tile_at(ref, r * ROW_SUB)


def _merge_kernel(oa_ref, ob_ref, gate_ref, x_ref, woa_ref, wob_ref, wout_ref, g2_ref, wrh_ref, wrl_ref, br_ref,
                  x1_ref, h2_ref, idx_ref, gw_ref, rank_ref, cnt_ref, carry_ref):
    i = pl.program_id(0)
    tm, d = x_ref.shape

    ga = gate_ref[:, :d].astype(F32)
    gb = gate_ref[:, d:].astype(F32)
    merged = ga * _dot(oa_ref[...], woa_ref[...]) + gb * _dot(ob_ref[...], wob_ref[...])
    x1 = x_ref[...] + _dot(merged.astype(BF16), wout_ref[...])
    x1_ref[...] = x1
    h2 = x1 * lax.rsqrt(jnp.mean(x1 * x1, axis=-1, keepdims=True) + EPS) * g2_ref[...]
    _store_row_tiles(h2_ref, h2)

    h_hi = h2.astype(BF16)
    h_lo = (h2 - h_hi.astype(F32)).astype(BF16)
    logits = _dot(h_hi, wrh_ref[...]) + _dot(h_lo, wrh_ref[...]) + _dot(h_hi, wrl_ref[...]) + br_ref[...]
    lane = _lane_iota((tm, LANES))
    lanef = lane.astype(F32)
    logits = jnp.where(lane < N_EXPERTS, logits, NEG)

    sels, vals = [], []
    for _ in range(TOP_K):
        mx = jnp.max(logits, axis=-1, keepdims=True)
        ix = jnp.min(jnp.where(logits == mx, lanef, float(LANES)), axis=-1, keepdims=True)
        sel = lanef == ix
        sels.append((sel, ix))
        vals.append(mx)
        logits = jnp.where(sel, 3.0 * NEG, logits)
    exps = [jnp.exp(v - vals[0]) for v in vals]
    den = exps[0] + exps[1] + exps[2] + exps[3]

    @pl.when(i == 0)
    def _():
        carry_ref[...] = jnp.zeros_like(carry_ref)

    onehot = jnp.zeros((tm, LANES), F32)
    for sel, _ in sels:
        onehot = onehot + sel.astype(F32)
    row = lax.broadcasted_iota(jnp.int32, (tm, tm), 0)
    col = lax.broadcasted_iota(jnp.int32, (tm, tm), 1)
    before = _dot((col < row).astype(BF16), onehot.astype(BF16)) + carry_ref[0:1, :]
    total = carry_ref[0:1, :] + jnp.sum(onehot, axis=0, keepdims=True)
    carry_ref[0:1, :] = total
    cnt_ref[...] = jnp.broadcast_to(total, cnt_ref.shape)

    idx_out = jnp.zeros((tm, LANES), F32)
    gw_out = jnp.zeros((tm, LANES), F32)
    rank_out = jnp.zeros((tm, LANES), F32)
    for k, (sel, ix) in enumerate(sels):
        slot = lane == k
        rank = jnp.sum(jnp.where(sel, before, 0.0), axis=-1, keepdims=True)
        idx_out = jnp.where(slot, ix, idx_out)
        gw_out = jnp.where(slot, exps[k] / den, gw_out)
        rank_out = jnp.where(slot, rank, rank_out)
    idx_ref[...] = idx_out.astype(jnp.int32)
    gw_ref[...] = gw_out
    rank_ref[...] = rank_out.astype(jnp.int32)


def _merge(out_a, out_b, gates, x2, w_o_swa, w_o_fox, w_out, norm2_g, w_router, b_router):
    n, d = x2.shape
    tm = MERGE_TM
    wr = jnp.pad(w_router, ((0, 0), (0, LANES - N_EXPERTS)))
    wrh = wr.astype(BF16)
    wrl = (wr - wrh.astype(F32)).astype(BF16)
    br = jnp.pad(b_router, (0, LANES - N_EXPERTS)).reshape(1, LANES)
    consts = (w_o_swa.astype(BF16), w_o_fox.astype(BF16), w_out.astype(BF16), norm2_g.reshape(1, d), wrh, wrl, br)
    row = lambda w: pl.BlockSpec((tm, w), lambda i: (i, 0))
    full = lambda a: pl.BlockSpec(a.shape, lambda i: (0,) * a.ndim)
    return pl.pallas_call(
        _merge_kernel,
        grid=(n // tm,),
        in_specs=[row(out_a.shape[1]), row(out_b.shape[1]), row(gates.shape[1]), row(d)] + [full(a) for a in consts],
        out_specs=[row(d), pl.BlockSpec((tm * ROW_SUB, LANES), lambda i: (i, 0)), row(LANES), row(LANES), row(LANES),
                   pl.BlockSpec((8, LANES), lambda i: (0, 0))],
        out_shape=[
            jax.ShapeDtypeStruct((n, d), F32),
            jax.ShapeDtypeStruct((n * ROW_SUB, LANES), F32),
            jax.ShapeDtypeStruct((n, LANES), jnp.int32),
            jax.ShapeDtypeStruct((n, LANES), F32),
            jax.ShapeDtypeStruct((n, LANES), jnp.int32),
            jax.ShapeDtypeStruct((8, LANES), F32),
        ],
        scratch_shapes=[pltpu.VMEM((8, LANES), F32)],
        compiler_params=_params(("arbitrary",)),
        name="merge",
    )(out_a, out_b, gates, x2, *consts)


def _idx_copy(dest_ref, idx_ref, sem_ref, tile, slot, per_tile):
    src = dest_ref.at[pl.ds(pl.multiple_of(tile * per_tile, per_tile), per_tile)]
    dst = idx_ref.at[pl.ds(pl.multiple_of(slot * per_tile, per_tile), per_tile)]
    return pltpu.make_async_copy(src, dst, sem_ref.at[slot])


def _dispatch_kernel(tail_start_ref, tail_flag_ref, n_used_ref, dest_ref, h_ref, xin_ref, idx_ref, zeros_ref, idx_sem,
                     row_sem, zero_sem):
    i = pl.program_id(0)
    n_tiles = pl.num_programs(0)
    t = h_ref.shape[0] // ROW_SUB
    per_tile = t * TOP_K
    slot = i % 2
    tm = zeros_ref.shape[0] // ROW_SUB
    n_blocks = xin_ref.shape[0] // zeros_ref.shape[0]

    def zero_copy(start):
        start = pl.multiple_of(start * ROW_SUB, ROW_SUB)
        return pltpu.make_async_copy(zeros_ref, xin_ref.at[pl.ds(start, tm * ROW_SUB)], zero_sem)

    def unused_block(op):
        def body(blk, carry):
            op(zero_copy(blk * tm))
            return carry
        lax.fori_loop(n_used_ref[0], n_blocks, body, 0)

    @pl.when(i == 0)
    def _():
        zeros_ref[...] = jnp.zeros_like(zeros_ref)
        _idx_copy(dest_ref, idx_ref, idx_sem, 0, 0, per_tile).start()
        for e in range(N_EXPERTS):
            @pl.when(tail_flag_ref[e] > 0)
            def _():
                zero_copy(tail_start_ref[e]).start()
        unused_block(lambda c: c.start())
        for e in range(N_EXPERTS):
            @pl.when(tail_flag_ref[e] > 0)
            def _():
                zero_copy(tail_start_ref[e]).wait()
        unused_block(lambda c: c.wait())

    @pl.when(i + 1 < n_tiles)
    def _():
        _idx_copy(dest_ref, idx_ref, idx_sem, i + 1, 1 - slot, per_tile).start()

    _idx_copy(dest_ref, idx_ref, idx_sem, i, slot, per_tile).wait()

    base = slot * per_tile

    def row_copy(r, k):
        return pltpu.make_async_copy(_row_tile(h_ref, r), _tile_at(xin_ref, idx_ref[base + r * TOP_K + k]), row_sem)

    def issue(r, carry):
        for k in range(TOP_K):
            row_copy(r, k).start(priority=k % 2)
        return carry

    def drain(r, carry):
        for k in range(TOP_K):
            row_copy(r, k).wait()
        return carry

    lax.fori_loop(0, t, issue, 0, unroll=8)
    lax.fori_loop(0, t, drain, 0, unroll=8)


def _dispatch(dest, h2t, tail_start, tail_flag, n_used, cap):
    n = h2t.shape[0] // ROW_SUB
    t = DISPATCH_T
    return pl.pallas_call(
        _dispatch_kernel,
        grid_spec=pltpu.PrefetchScalarGridSpec(
            num_scalar_prefetch=3,
            grid=(n // t,),
            in_specs=[pl.BlockSpec(memory_space=pl.ANY), pl.BlockSpec((t * ROW_SUB, LANES), lambda i, *_: (i, 0))],
            out_specs=pl.BlockSpec(memory_space=pl.ANY),
            scratch_shapes=[
                pltpu.SMEM((2 * t * TOP_K,), jnp.int32),
                pltpu.VMEM((EXPERT_TM * ROW_SUB, LANES), F32),
                pltpu.SemaphoreType.DMA((2,)),
                pltpu.SemaphoreType.DMA,
                pltpu.SemaphoreType.DMA,
            ],
        ),
        out_shape=jax.ShapeDtypeStruct((cap * ROW_SUB, LANES), F32),
        compiler_params=_params(("arbitrary",)),
        name="dispatch",
    )(tail_start, tail_flag, n_used, dest, h2t)


def _expert_kernel(blk_expert_ref, n_used_ref, x_ref, wgu_ref, bgu_ref, wd_ref, bd_ref, y_ref, wgu_bf, wd_bf):
    i = pl.program_id(0)
    f = wd_ref.shape[1]
    used = i < n_used_ref[0]

    new_expert = (i == 0) | (blk_expert_ref[i] != blk_expert_ref[jnp.maximum(i - 1, 0)])

    @pl.when(used & new_expert)
    def _():
        step = EXPERT_FC
        for r in range(0, wgu_ref.shape[1], step):
            wgu_bf[r:r + step, :] = wgu_ref[0, r:r + step, :].astype(BF16)
        for r in range(0, f, step):
            wd_bf[r:r + step, :] = wd_ref[0, r:r + step, :].astype(BF16)

    @pl.when(used)
    def _():
        x = _load_row_tiles(x_ref).astype(BF16)
        y = bd_ref[0]
        for c in range(f // EXPERT_FC):
            lo, hi = c * EXPERT_FC, (c + 1) * EXPERT_FC
            g = _dot(x, wgu_bf[:, lo:hi]) + bgu_ref[0, :, lo:hi]
            u = _dot(x, wgu_bf[:, f + lo:f + hi]) + bgu_ref[0, :, f + lo:f + hi]
            g = jnp.minimum(g, SWIGLU_LIMIT)
            u = jnp.clip(u, -SWIGLU_LIMIT, SWIGLU_LIMIT)
            a = (u + 1.0) * (g * (1.0 / (1.0 + jnp.exp(-SWIGLU_ALPHA * g))))
            y = y + _dot(a.astype(BF16), wd_bf[lo:hi, :])
        _store_row_tiles(y_ref, y)

    @pl.when(i >= n_used_ref[0])
    def _():
        y_ref[...] = jnp.zeros_like(y_ref)


def _experts(xin, blk_expert, n_used, w_gate_up, b_gate_up, w_down, b_down):
    cap = xin.shape[0] // ROW_SUB
    tm = EXPERT_TM
    ne, d, f2 = w_gate_up.shape
    f = f2 // 2
    rows = lambda i, be, nu: (jnp.minimum(i, nu[0] - 1), 0)
    by_expert = lambda i, be, nu: (be[i], 0, 0)
    return pl.pallas_call(
        _expert_kernel,
        grid_spec=pltpu.PrefetchScalarGridSpec(
            num_scalar_prefetch=2,
            grid=(cap // tm,),
            in_specs=[
                pl.BlockSpec((tm * ROW_SUB, LANES), rows),
                pl.BlockSpec((1, d, f2), by_expert),
                pl.BlockSpec((1, 1, f2), by_expert),
                pl.BlockSpec((1, f, d), by_expert),
                pl.BlockSpec((1, 1, d), by_expert),
            ],
            out_specs=pl.BlockSpec((tm * ROW_SUB, LANES), lambda i, be, nu: (i, 0)),
            scratch_shapes=[pltpu.VMEM((d, f2), BF16), pltpu.VMEM((f, d), BF16)],
        ),
        out_shape=jax.ShapeDtypeStruct((cap * ROW_SUB, LANES), F32),
        compiler_params=_params(("arbitrary",)),
        name="experts",
    )(blk_expert, n_used, xin, w_gate_up, b_gate_up.reshape(ne, 1, f2), w_down, b_down.reshape(ne, 1, d))


def _combine_kernel(dest_ref, y_ref, x1_ref, gw_ref, o_ref, idx_ref, rows_ref, idx_sem, row_sem):
    i = pl.program_id(0)
    n_tiles = pl.num_programs(0)
    t = x1_ref.shape[0]
    per_tile = t * TOP_K
    slot = i % 2

    def row_copy(tile_slot, r, k, d):
        dst = _row_tile(rows_ref, (tile_slot * TOP_K + k) * t + r)
        return pltpu.make_async_copy(_tile_at(y_ref, d), dst, row_sem.at[tile_slot])

    def gather(tile_slot):
        base = tile_slot * per_tile

        def issue(r, carry):
            for k in range(TOP_K):
                row_copy(tile_slot, r, k, idx_ref[base + r * TOP_K + k]).start(priority=k % 2)
            return carry

        lax.fori_loop(0, t, issue, 0, unroll=8)

    @pl.when(i == 0)
    def _():
        _idx_copy(dest_ref, idx_ref, idx_sem, 0, 0, per_tile).start()
        _idx_copy(dest_ref, idx_ref, idx_sem, 0, 0, per_tile).wait()
        gather(0)

        @pl.when(n_tiles > 1)
        def _():
            _idx_copy(dest_ref, idx_ref, idx_sem, 1, 1, per_tile).start()

    @pl.when(i + 1 < n_tiles)
    def _():
        _idx_copy(dest_ref, idx_ref, idx_sem, i + 1, 1 - slot, per_tile).wait()
        gather(1 - slot)

    @pl.when(i + 2 < n_tiles)
    def _():
        _idx_copy(dest_ref, idx_ref, idx_sem, i + 2, slot, per_tile).start()

    def drain(r, carry):
        for k in range(TOP_K):
            row_copy(slot, r, k, 0).wait()
        return carry

    lax.fori_loop(0, t, drain, 0, unroll=8)

    weights = [jnp.broadcast_to(gw_ref[:, k:k + 1], (t, LANES)) for k in range(TOP_K)]
    for j in range(ROW_SUB):
        acc = x1_ref[:, j * LANES:(j + 1) * LANES]
        for k in range(TOP_K):
            acc = acc + weights[k] * rows_ref[_group(rows_ref, j, t, base=(slot * TOP_K + k) * t)]
        o_ref[:, j * LANES:(j + 1) * LANES] = acc


def _combine(dest, yb, x1, gw):
    n, d = x1.shape
    t = COMBINE_T
    return pl.pallas_call(
        _combine_kernel,
        grid=(n // t,),
        in_specs=[
            pl.BlockSpec(memory_space=pl.ANY),
            pl.BlockSpec(memory_space=pl.ANY),
            pl.BlockSpec((t, d), lambda i: (i, 0)),
            pl.BlockSpec((t, LANES), lambda i: (i, 0)),
        ],
        out_specs=pl.BlockSpec((t, d), lambda i: (i, 0)),
        out_shape=jax.ShapeDtypeStruct((n, d), F32),
        scratch_shapes=[
            pltpu.SMEM((2 * t * TOP_K,), jnp.int32),
            pltpu.VMEM((2 * TOP_K * t * ROW_SUB, LANES), F32),
            pltpu.SemaphoreType.DMA((2,)),
            pltpu.SemaphoreType.DMA((2,)),
        ],
        compiler_params=_params(("arbitrary",)),
        name="combine",
    )(dest, yb, x1, gw)


def _layer(x, positions, norm1_g, w_in, b_forget, b_gate, q_norm_swa, k_norm_swa, sinks, q_norm_fox, k_norm_fox,
           w_o_swa, w_o_fox, w_out, norm2_g, w_router, b_router, w_gate_up, b_gate_up, w_down, b_down):
    b, s, d = x.shape
    n = b * s
    assert s % max(PROJ_TM, SWA_TQ, FOX_TQ) == 0 and n % max(MERGE_TM, DISPATCH_T, COMBINE_T) == 0
    assert d == ROW_SUB * LANES
    x2 = x.reshape(n, d)

    qa, ka2, va2, qf, kf, vf, gates = _in_proj(x2, positions.reshape(n, 1), norm1_g, w_in, b_forget, b_gate,
                                               q_norm_swa, k_norm_swa, q_norm_fox, k_norm_fox, s)
    seq = lambda a: a.reshape(b, s, a.shape[1])
    out_a = _swa(seq(qa), seq(ka2), seq(va2), sinks).reshape(n, SWA_Q_W)
    out_b = _fox(seq(qf), seq(kf), seq(vf)).reshape(n, FOX_W)

    x1, h2t, idx, gw, rank, totals = _merge(out_a, out_b, gates, x2, w_o_swa, w_o_fox, w_out, norm2_g, w_router,
                                            b_router)

    tm = EXPERT_TM
    cap = n * TOP_K + N_EXPERTS * tm
    counts = totals[0, :N_EXPERTS].astype(jnp.int32)
    padded = (counts + tm - 1) // tm * tm
    pad_end = jnp.cumsum(padded)
    pad_start = pad_end - padded
    dest = ((pad_start[idx[:, :TOP_K]] + rank[:, :TOP_K]) * ROW_SUB).reshape(n * TOP_K)
    blk_start = jnp.arange(cap // tm, dtype=jnp.int32) * tm
    blk_expert = jnp.minimum(jnp.sum(pad_end[None, :] <= blk_start[:, None], axis=1), N_EXPERTS - 1).astype(jnp.int32)
    n_used = (pad_end[-1:] // tm).astype(jnp.int32)

    xin = _dispatch(dest, h2t, (pad_end - tm).astype(jnp.int32), (padded > 0).astype(jnp.int32), n_used, cap)
    yb = _experts(xin, blk_expert, n_used, w_gate_up, b_gate_up, w_down, b_down)
    return _combine(dest, yb, x1, gw).reshape(b, s, d)


def kernel(x, positions, norm1_g, w_in, b_forget, b_gate, q_norm_swa, k_norm_swa, sinks, q_norm_fox, k_norm_fox,
           w_o_swa, w_o_fox, w_out, norm2_g, w_router, b_router, w_gate_up, b_gate_up, w_down, b_down):
    for l in range(norm1_g.shape[0]):
        x = _layer(x, positions, norm1_g[l], w_in[l], b_forget[l], b_gate[l], q_norm_swa[l], k_norm_swa[l],
                   sinks[l], q_norm_fox[l], k_norm_fox[l], w_o_swa[l], w_o_fox[l], w_out[l], norm2_g[l],
                   w_router[l], b_router[l], w_gate_up[l], b_gate_up[l], w_down[l], b_down[l])
    return x
```

```python
import functools

import jax
import jax.numpy as jnp
import numpy as np
from jax import lax
from jax.experimental import pallas as pl
from jax.experimental.pallas import tpu as pltpu

HEAD_DIM = 64
SWA_Q_HEADS = 8
SWA_KV_HEADS = 2
SWA_BLOCK = 128
FOX_HEADS = 8
ROPE_THETA = 10000.0
N_EXPERTS = 32
TOP_K = 4
SWIGLU_LIMIT = 7.0
SWIGLU_ALPHA = 1.702
EPS = 1e-6
NEG = -1e30
ATTN_SCALE = HEAD_DIM ** -0.5
LOG2E = 1.4426950408889634

LANES = 128
VMEM_LIMIT_BYTES = 56 * 1024 * 1024

SWA_Q_W = SWA_Q_HEADS * HEAD_DIM
SWA_KV_W = SWA_KV_HEADS * HEAD_DIM
FOX_W = FOX_HEADS * HEAD_DIM
FOX_AUG_W = FOX_HEADS * LANES

PROJ_TM = 512
SWA_TQ = 512
FOX_TQ = 1024
FOX_SUB = 512
FOX_TK = 512
MERGE_TM = 512
DISPATCH_T = 256
EXPERT_TM = 512
EXPERT_FC = 512
COMBINE_T = 256

BF16 = jnp.bfloat16
F32 = jnp.float32


def _params(sem):
    return pltpu.CompilerParams(dimension_semantics=sem, vmem_limit_bytes=VMEM_LIMIT_BYTES)


def _dot(a, b):
    return jnp.dot(a, b, preferred_element_type=F32)


def _dot_nt(a, b):
    return lax.dot_general(a, b, (((1,), (1,)), ((), ())), preferred_element_type=F32)


def _split3(v):
    hi = v.astype(BF16)
    r1 = v - hi.astype(F32)
    mid = r1.astype(BF16)
    lo = (r1 - mid.astype(F32)).astype(BF16)
    return hi, mid, lo


def _sigmoid(z):
    return 0.5 * jnp.tanh(0.5 * z) + 0.5


def _pack3(v):
    hi, mid, lo = _split3(v)
    packed = hi.astype(F32) + pltpu.roll(mid.astype(F32), FOX_HEADS, 1) + pltpu.roll(lo.astype(F32), 2 * FOX_HEADS, 1)
    return packed.astype(BF16)


def _lane_iota(shape):
    return lax.broadcasted_iota(jnp.int32, shape, len(shape) - 1)


def _head_norm(t, gain, bd):
    ss = _dot((t * t).astype(BF16), bd)
    return t * lax.rsqrt(ss * (1.0 / HEAD_DIM) + EPS) * gain


def _rope(t, cos, sin_signed, first_half):
    partner = jnp.where(first_half, pltpu.roll(t, LANES - HEAD_DIM // 2, 1), pltpu.roll(t, HEAD_DIM // 2, 1))
    return t * cos + partner * sin_signed


def _in_proj_kernel(tiles_per_batch, x_ref, pos_ref, g1_ref, wqkv_ref, wfl_ref, wgl_ref, bfl_ref, bgl_ref,
                    gqa_ref, gka_ref, gqf_ref, gkf_ref, invf_ref, eq_ref, ek_ref, oneq_ref, onek_ref,
                    qa_ref, ka_ref, va_ref, qf_ref, kf_ref, vf_ref, gate_ref, carry_ref):
    i = pl.program_id(0)
    tm = x_ref.shape[0]

    x = x_ref[...]
    h = x * lax.rsqrt(jnp.mean(x * x, axis=-1, keepdims=True) + EPS) * g1_ref[...]
    h = h.astype(BF16)

    lane = _lane_iota((tm, LANES))
    in_low_head = lane < HEAD_DIM
    first_half = (lane & (HEAD_DIM - 1)) < (HEAD_DIM // 2)
    shift = HEAD_DIM.bit_length() - 1
    rr = lax.broadcasted_iota(jnp.int32, (2 * LANES, 2 * LANES), 0) >> shift
    cc = lax.broadcasted_iota(jnp.int32, (2 * LANES, 2 * LANES), 1) >> shift
    bd2 = (rr == cc).astype(BF16)
    bd = bd2[:LANES, :LANES]

    ang = pos_ref[...].astype(F32) * invf_ref[...]
    cos = jnp.cos(ang)
    sin_signed = jnp.where(first_half, -jnp.sin(ang), jnp.sin(ang))

    fl = _dot(h, wfl_ref[...]) + bfl_ref[...]
    logf = jnp.minimum(fl, 0.0) - jnp.log(1.0 + jnp.exp(-jnp.abs(fl)))
    logf = jnp.where(lane < FOX_HEADS, logf, 0.0)
    row = lax.broadcasted_iota(jnp.int32, (tm, tm), 0)
    col = lax.broadcasted_iota(jnp.int32, (tm, tm), 1)
    tri = (col <= row).astype(BF16)
    c3 = _dot(tri, _pack3(logf))
    csum = c3 + pltpu.roll(c3, LANES - FOX_HEADS, 1) + pltpu.roll(c3, LANES - 2 * FOX_HEADS, 1)
    csum = jnp.where(lane < FOX_HEADS, csum, 0.0)

    @pl.when(i % tiles_per_batch == 0)
    def _():
        carry_ref[...] = jnp.zeros_like(carry_ref)

    csum = csum + carry_ref[0:1, :]
    carry_ref[0:1, :] = csum[tm - 1:tm, :]
    c_parts = _pack3(csum * LOG2E)

    wide = 2 * LANES
    for c in range(SWA_Q_W // wide):
        t = _dot(h, wqkv_ref[:, c * wide:(c + 1) * wide])
        t = _head_norm(t, gqa_ref[:, c * wide:(c + 1) * wide], bd2)
        for g in range(2):
            lo = c * wide + g * LANES
            qa_ref[:, lo:lo + LANES] = _rope(t[:, g * LANES:(g + 1) * LANES], cos, sin_signed, first_half).astype(BF16)
    off = SWA_Q_W
    t = _dot(h, wqkv_ref[:, off:off + LANES])
    t = _rope(_head_norm(t, gka_ref[...], bd), cos, sin_signed, first_half)
    ka_ref[:, 0:LANES] = t.astype(BF16)
    ka_ref[:, LANES:2 * LANES] = pltpu.roll(t, HEAD_DIM, 1).astype(BF16)
    off += SWA_KV_W
    t = _dot(h, wqkv_ref[:, off:off + LANES])
    va_ref[:, 0:LANES] = t.astype(BF16)
    va_ref[:, LANES:2 * LANES] = pltpu.roll(t, HEAD_DIM, 1).astype(BF16)
    off += SWA_KV_W

    aug_q = _dot(c_parts, eq_ref[...]) + oneq_ref[...]
    aug_k = _dot(c_parts, ek_ref[...]) + onek_ref[...]
    for name_off, gain_ref, aug, out_ref in ((off, gqf_ref, aug_q, qf_ref), (off + FOX_W, gkf_ref, aug_k, kf_ref)):
        for c in range(FOX_W // wide):
            t2 = _dot(h, wqkv_ref[:, name_off + c * wide:name_off + (c + 1) * wide])
            t2 = _head_norm(t2, gain_ref[:, c * wide:(c + 1) * wide], bd2)
            for g in range(2):
                t = t2[:, g * LANES:(g + 1) * LANES]
                even = 2 * (2 * c + g) * LANES
                odd = even + LANES
                out_ref[:, even:even + LANES] = jnp.where(in_low_head, t, aug[:, even:even + LANES]).astype(BF16)
                out_ref[:, odd:odd + LANES] = jnp.where(in_low_head, aug[:, odd:odd + LANES], t).astype(BF16)
    off += 2 * FOX_W
    for c in range(FOX_W // LANES):
        t = _dot(h, wqkv_ref[:, off + c * LANES:off + (c + 1) * LANES])
        vf_ref[:, 2 * c * LANES:(2 * c + 1) * LANES] = jnp.where(in_low_head, t, 1.0).astype(BF16)
        vf_ref[:, (2 * c + 1) * LANES:(2 * c + 2) * LANES] = jnp.where(in_low_head, 1.0, t).astype(BF16)

    gw = 512
    for c in range(gate_ref.shape[1] // gw):
        z = _dot(h, wgl_ref[:, c * gw:(c + 1) * gw]) + bgl_ref[:, c * gw:(c + 1) * gw]
        gate_ref[:, c * gw:(c + 1) * gw] = _sigmoid(z).astype(BF16)


def _aug_placement():
    eq = np.zeros((LANES, FOX_AUG_W), np.float32)
    ek = np.zeros((LANES, FOX_AUG_W), np.float32)
    oneq = np.zeros((1, FOX_AUG_W), np.float32)
    onek = np.zeros((1, FOX_AUG_W), np.float32)
    for h in range(FOX_HEADS):
        base = h * LANES + (HEAD_DIM if h % 2 == 0 else 0)
        for p in range(3):
            eq[p * FOX_HEADS + h, base + p] = 1.0
            ek[p * FOX_HEADS + h, base + 3 + p] = -1.0
            oneq[0, base + 3 + p] = 1.0
            onek[0, base + p] = 1.0
    return jnp.asarray(eq, BF16), jnp.asarray(ek, BF16), jnp.asarray(oneq), jnp.asarray(onek)


def _in_proj(x2, pos2, norm1_g, w_in, b_forget, b_gate, q_norm_swa, k_norm_swa, q_norm_fox, k_norm_fox, seq):
    n, d = x2.shape
    tm = PROJ_TM
    n_qkv = SWA_Q_W + 2 * SWA_KV_W + 3 * FOX_W
    wqkv = w_in[:, :n_qkv].astype(BF16)
    wfl = jnp.pad(w_in[:, n_qkv:n_qkv + FOX_HEADS], ((0, 0), (0, LANES - FOX_HEADS))).astype(BF16)
    wgl = w_in[:, n_qkv + FOX_HEADS:].astype(BF16)
    n_gate = wgl.shape[1]
    bfl = jnp.pad(b_forget, (0, LANES - FOX_HEADS)).reshape(1, LANES)
    bgl = b_gate.reshape(1, n_gate)
    gqa = (jnp.tile(q_norm_swa, SWA_Q_HEADS) * ATTN_SCALE).reshape(1, SWA_Q_W)
    gka = jnp.tile(k_norm_swa, SWA_KV_HEADS).reshape(1, SWA_KV_W)
    gqf = (jnp.tile(q_norm_fox, FOX_HEADS) * (ATTN_SCALE * LOG2E)).reshape(1, FOX_W)
    gkf = jnp.tile(k_norm_fox, FOX_HEADS).reshape(1, FOX_W)
    half = HEAD_DIM // 2
    inv = ROPE_THETA ** (-jnp.arange(half, dtype=F32) / half)
    invf = jnp.tile(inv, LANES // half).reshape(1, LANES)
    eq, ek, oneq, onek = _aug_placement()

    row = lambda w: pl.BlockSpec((tm, w), lambda i: (i, 0))
    full = lambda a: pl.BlockSpec(a.shape, lambda i: (0,) * a.ndim)
    consts = (norm1_g.reshape(1, d), wqkv, wfl, wgl, bfl, bgl, gqa, gka, gqf, gkf, invf, eq, ek, oneq, onek)
    out_widths = (SWA_Q_W, 2 * SWA_KV_W, 2 * SWA_KV_W, FOX_AUG_W, FOX_AUG_W, 2 * FOX_W, n_gate)
    return pl.pallas_call(
        functools.partial(_in_proj_kernel, seq // tm),
        grid=(n // tm,),
        in_specs=[row(d), row(1)] + [full(a) for a in consts],
        out_specs=[row(w) for w in out_widths],
        out_shape=[jax.ShapeDtypeStruct((n, w), BF16) for w in out_widths],
        scratch_shapes=[pltpu.VMEM((8, LANES), F32)],
        compiler_params=_params(("arbitrary",)),
        name="in_proj",
    )(x2, pos2, *consts)


def _swa_kernel(sinks_ref, q_ref, kc_ref, kp_ref, vc_ref, vp_ref, o_ref):
    i = pl.program_id(1)
    tq = q_ref.shape[1]
    blk = SWA_BLOCK
    lane = _lane_iota((2 * blk, LANES))
    low = lane < HEAD_DIM
    qi = lax.broadcasted_iota(jnp.int32, (blk, 2 * blk), 0)
    kj = lax.broadcasted_iota(jnp.int32, (blk, 2 * blk), 1)
    cur_ok = (kj >= blk) & (kj - blk <= qi)
    prev_ok = (kj < blk) & (kj > qi)
    zero = jnp.zeros((2 * blk, LANES), BF16)
    group = SWA_Q_HEADS // SWA_KV_HEADS

    for j in range(tq // blk):
        if j == 0:
            k_prev, v_prev = kp_ref[0], vp_ref[0]
            valid = cur_ok | (prev_ok & (i > 0))
        else:
            k_prev, v_prev = kc_ref[0, (j - 1) * blk:j * blk, :], vc_ref[0, (j - 1) * blk:j * blk, :]
            valid = cur_ok | prev_ok
        kwin = jnp.concatenate([k_prev, kc_ref[0, j * blk:(j + 1) * blk, :]], axis=0)
        vwin = jnp.concatenate([v_prev, vc_ref[0, j * blk:(j + 1) * blk, :]], axis=0)
        nat_k, swp_k = kwin[:, :LANES], kwin[:, LANES:]
        nat_v, swp_v = vwin[:, :LANES], vwin[:, LANES:]
        k_var = ((jnp.where(low, nat_k, zero), jnp.where(low, zero, swp_k)),
                 (jnp.where(low, swp_k, zero), jnp.where(low, zero, nat_k)))
        v_var = ((jnp.where(low, nat_v, zero), jnp.where(low, zero, swp_v)),
                 (jnp.where(low, swp_v, zero), jnp.where(low, zero, nat_v)))
        for c in range(SWA_Q_W // LANES):
            qc = q_ref[0, j * blk:(j + 1) * blk, c * LANES:(c + 1) * LANES]
            out = jnp.zeros((blk, LANES), F32)
            for half in range(2):
                head = 2 * c + half
                kv = head // group
                s = jnp.where(valid, _dot_nt(qc, k_var[kv][half]), NEG)
                sink = sinks_ref[head]
                m = jnp.maximum(jnp.max(s, axis=-1, keepdims=True), sink)
                p = jnp.exp(s - m)
                den = jnp.sum(p, axis=-1, keepdims=True) + jnp.exp(sink - m)
                out = out + _dot(p.astype(BF16), v_var[kv][half]) / den
            o_ref[0, j * blk:(j + 1) * blk, c * LANES:(c + 1) * LANES] = out.astype(BF16)


def _swa(qa, ka2, va2, sinks):
    b, s, _ = qa.shape
    tq = SWA_TQ
    per = tq // SWA_BLOCK
    cur = lambda w: pl.BlockSpec((1, tq, w), lambda bi, i, sk: (bi, i, 0))
    prev = lambda w: pl.BlockSpec((1, SWA_BLOCK, w), lambda bi, i, sk: (bi, jnp.maximum(i * per - 1, 0), 0))
    return pl.pallas_call(
        _swa_kernel,
        grid_spec=pltpu.PrefetchScalarGridSpec(
            num_scalar_prefetch=1,
            grid=(b, s // tq),
            in_specs=[cur(SWA_Q_W), cur(2 * SWA_KV_W), prev(2 * SWA_KV_W), cur(2 * SWA_KV_W), prev(2 * SWA_KV_W)],
            out_specs=cur(SWA_Q_W),
        ),
        out_shape=jax.ShapeDtypeStruct((b, s, SWA_Q_W), BF16),
        compiler_params=_params(("arbitrary", "arbitrary")),
        name="swa",
    )(sinks, qa, ka2, ka2, va2, va2)


def _fox_kernel(q_ref, k_ref, v_ref, o_ref, m_ref, acc_ref, s_ref):
    qi = pl.program_id(2)
    sub, tk = FOX_SUB, FOX_TK
    tq = q_ref.shape[1]
    n_sub = tq // sub
    n_chunk = tk // LANES
    row = lax.broadcasted_iota(jnp.int32, (sub, LANES), 0)
    col = lax.broadcasted_iota(jnp.int32, (sub, LANES), 1)
    chains = [(half, sb) for sb in range(n_sub) for half in range(2)]

    m_ref[...] = jnp.full_like(m_ref, NEG)
    acc_ref[...] = jnp.zeros_like(acc_ref)

    def scores(buf, chain, kb):
        half, sb = chains[chain]
        start = pl.multiple_of(kb * tk, tk)
        q = q_ref[0, sb * sub:(sb + 1) * sub, half * LANES:(half + 1) * LANES]
        k = k_ref[0, pl.ds(start, tk), half * LANES:(half + 1) * LANES]
        s_ref[buf, chain] = _dot_nt(q, k)

    def fold(buf, chain, kb, mask_offset):
        half, sb = chains[chain]
        start = pl.multiple_of(kb * tk, tk)
        v = v_ref[0, pl.ds(start, tk), half * LANES:(half + 1) * LANES]
        s = [s_ref[buf, chain, :, c * LANES:(c + 1) * LANES] for c in range(n_chunk)]
        if mask_offset is not None:
            s = [jnp.where(col + (c * LANES + mask_offset) <= row, s[c], NEG) for c in range(n_chunk)]
        m_prev = m_ref[chain]
        peak = functools.reduce(jnp.maximum, s)
        m_new = jnp.maximum(m_prev, jnp.max(peak, axis=-1, keepdims=True))
        alpha = jnp.exp2(m_prev - m_new)
        p = [jnp.exp2(sc - m_new) for sc in s]
        pv = _dot(jnp.concatenate(p, axis=1).astype(BF16), v)
        acc_ref[chain] = alpha * acc_ref[chain] + pv
        m_ref[chain] = m_new

    assert tq // tk == 2
    every = range(len(chains))
    for chain in every:
        scores(0, chain, 0)

    def body(t, carry):
        for chain in every:
            scores(1, chain, 2 * t + 1)
        for chain in every:
            fold(0, chain, 2 * t, None)
        for chain in every:
            scores(0, chain, 2 * t + 2)
        for chain in every:
            fold(1, chain, 2 * t + 1, None)
        return carry

    lax.fori_loop(0, qi, body, 0)

    def visibility(j, sb):
        first_key, first_row = j * tk, sb * sub
        if first_key + tk - 1 <= first_row:
            return True, None
        if first_key <= first_row + sub - 1:
            return True, first_key - first_row
        return False, None

    for chain, (half, sb) in enumerate(chains):
        if visibility(1, sb)[0]:
            scores(1, chain, 2 * qi + 1)
    for j in range(2):
        for chain, (half, sb) in enumerate(chains):
            visible, mask_offset = visibility(j, sb)
            if visible:
                fold(j, chain, 2 * qi + j, mask_offset)

    low = _lane_iota((sub, LANES)) < HEAD_DIM
    for sb in range(n_sub):
        even, odd = acc_ref[2 * sb], acc_ref[2 * sb + 1]
        even = even / pltpu.roll(even, HEAD_DIM, 1)
        odd = odd / pltpu.roll(odd, HEAD_DIM, 1)
        o_ref[0, sb * sub:(sb + 1) * sub, :] = jnp.where(low, even, odd).astype(BF16)


def _fox(qf, kf, vf):
    b, s, _ = qf.shape
    tq = FOX_TQ
    pairs = FOX_HEADS // 2
    n_chain = 2 * (tq // FOX_SUB)
    return pl.pallas_call(
        _fox_kernel,
        grid=(b, pairs, s // tq),
        in_specs=[
            pl.BlockSpec((1, tq, 2 * LANES), lambda bi, p, i: (bi, i, p)),
            pl.BlockSpec((1, s, 2 * LANES), lambda bi, p, i: (bi, 0, p)),
            pl.BlockSpec((1, s, 2 * LANES), lambda bi, p, i: (bi, 0, p)),
        ],
        out_specs=pl.BlockSpec((1, tq, LANES), lambda bi, p, i: (bi, i, p)),
        out_shape=jax.ShapeDtypeStruct((b, s, FOX_W), BF16),
        scratch_shapes=[pltpu.VMEM((n_chain, FOX_SUB, LANES), F32)] * 2
        + [pltpu.VMEM((2, n_chain, FOX_SUB, FOX_TK), F32)],
        compiler_params=_params(("arbitrary", "arbitrary", "arbitrary")),
        name="fox",
    )(qf, kf, vf)


ROW_SUB = 8


def _group(ref, j, rows, base=0):
    return (pl.ds(base * ROW_SUB + j, rows, stride=ROW_SUB), slice(None))


def _store_row_tiles(ref, v):
    for j in range(ROW_SUB):
        ref[_group(ref, j, v.shape[0])] = v[:, j * LANES:(j + 1) * LANES]


def _load_row_tiles(ref):
    rows = ref.shape[0] // ROW_SUB
    return jnp.concatenate([ref[_group(ref, j, rows)] for j in range(ROW_SUB)], axis=1)


def _tile_at(ref, first_sublane):
    return ref.at[pl.ds(pl.multiple_of(first_sublane, ROW_SUB), ROW_SUB)]


def _row_tile(ref, r):
    return _tile_at(ref, r * ROW_SUB)


def _merge_kernel(oa_ref, ob_ref, gate_ref, x_ref, woa_ref, wob_ref, wout_ref, g2_ref, wr2_ref, br_ref,
                  x1_ref, h2_ref, idx_ref, gw_ref, rank_ref, cnt_ref, carry_ref):
    i = pl.program_id(0)
    tm, d = x_ref.shape

    ga = gate_ref[:, :d].astype(F32)
    gb = gate_ref[:, d:].astype(F32)
    merged = ga * _dot(oa_ref[...], woa_ref[...]) + gb * _dot(ob_ref[...], wob_ref[...])
    x1 = x_ref[...] + _dot(merged.astype(BF16), wout_ref[...])
    x1_ref[...] = x1
    h2 = x1 * lax.rsqrt(jnp.mean(x1 * x1, axis=-1, keepdims=True) + EPS) * g2_ref[...]
    _store_row_tiles(h2_ref, h2)

    h_hi = h2.astype(BF16)
    h_lo = (h2 - h_hi.astype(F32)).astype(BF16)
    both = _dot(h_hi, wr2_ref[...])
    logits = both[:, :LANES] + both[:, LANES:] + _dot(h_lo, wr2_ref[:, :LANES]) + br_ref[...]
    lane = _lane_iota((tm, LANES))
    lanef = lane.astype(F32)
    logits = jnp.where(lane < N_EXPERTS, logits, NEG)

    sels, vals = [], []
    for _ in range(TOP_K):
        mx = jnp.max(logits, axis=-1, keepdims=True)
        ix = jnp.min(jnp.where(logits == mx, lanef, float(LANES)), axis=-1, keepdims=True)
        sel = lanef == ix
        sels.append((sel, ix))
        vals.append(mx)
        logits = jnp.where(sel, 3.0 * NEG, logits)
    exps = [jnp.exp(v - vals[0]) for v in vals]
    den = exps[0] + exps[1] + exps[2] + exps[3]

    @pl.when(i == 0)
    def _():
        carry_ref[...] = jnp.zeros_like(carry_ref)

    onehot = jnp.zeros((tm, LANES), F32)
    for sel, _ in sels:
        onehot = onehot + sel.astype(F32)
    row = lax.broadcasted_iota(jnp.int32, (tm, tm), 0)
    col = lax.broadcasted_iota(jnp.int32, (tm, tm), 1)
    before = _dot((col < row).astype(BF16), onehot.astype(BF16)) + carry_ref[0:1, :]
    total = carry_ref[0:1, :] + jnp.sum(onehot, axis=0, keepdims=True)
    carry_ref[0:1, :] = total
    cnt_ref[...] = jnp.broadcast_to(total, cnt_ref.shape)

    idx_out = jnp.zeros((tm, LANES), F32)
    gw_out = jnp.zeros((tm, LANES), F32)
    rank_out = jnp.zeros((tm, LANES), F32)
    for k, (sel, ix) in enumerate(sels):
        slot = lane == k
        rank = jnp.sum(jnp.where(sel, before, 0.0), axis=-1, keepdims=True)
        idx_out = jnp.where(slot, ix, idx_out)
        gw_out = jnp.where(slot, exps[k] / den, gw_out)
        rank_out = jnp.where(slot, rank, rank_out)
    idx_ref[...] = idx_out.astype(jnp.int32)
    gw_ref[...] = gw_out
    rank_ref[...] = rank_out.astype(jnp.int32)


def _merge(out_a, out_b, gates, x2, w_o_swa, w_o_fox, w_out, norm2_g, w_router, b_router):
    n, d = x2.shape
    tm = MERGE_TM
    wr = jnp.pad(w_router, ((0, 0), (0, LANES - N_EXPERTS)))
    wrh = wr.astype(BF16)
    wrl = (wr - wrh.astype(F32)).astype(BF16)
    wr2 = jnp.concatenate([wrh, wrl], axis=1)
    br = jnp.pad(b_router, (0, LANES - N_EXPERTS)).reshape(1, LANES)
    consts = (w_o_swa.astype(BF16), w_o_fox.astype(BF16), w_out.astype(BF16), norm2_g.reshape(1, d), wr2, br)
    row = lambda w: pl.BlockSpec((tm, w), lambda i: (i, 0))
    full = lambda a: pl.BlockSpec(a.shape, lambda i: (0,) * a.ndim)
    return pl.pallas_call(
        _merge_kernel,
        grid=(n // tm,),
        in_specs=[row(out_a.shape[1]), row(out_b.shape[1]), row(gates.shape[1]), row(d)] + [full(a) for a in consts],
        out_specs=[row(d), pl.BlockSpec((tm * ROW_SUB, LANES), lambda i: (i, 0)), row(LANES), row(LANES), row(LANES),
                   pl.BlockSpec((8, LANES), lambda i: (0, 0))],
        out_shape=[
            jax.ShapeDtypeStruct((n, d), F32),
            jax.ShapeDtypeStruct((n * ROW_SUB, LANES), F32),
            jax.ShapeDtypeStruct((n, LANES), jnp.int32),
            jax.ShapeDtypeStruct((n, LANES), F32),
            jax.ShapeDtypeStruct((n, LANES), jnp.int32),
            jax.ShapeDtypeStruct((8, LANES), F32),
        ],
        scratch_shapes=[pltpu.VMEM((8, LANES), F32)],
        compiler_params=_params(("arbitrary",)),
        name="merge",
    )(out_a, out_b, gates, x2, *consts)


def _idx_copy(dest_ref, idx_ref, sem_ref, tile, slot, per_tile):
    src = dest_ref.at[pl.ds(pl.multiple_of(tile * per_tile, per_tile), per_tile)]
    dst = idx_ref.at[pl.ds(pl.multiple_of(slot * per_tile, per_tile), per_tile)]
    return pltpu.make_async_copy(src, dst, sem_ref.at[slot])


def _dispatch_kernel(tail_start_ref, tail_flag_ref, n_used_ref, dest_ref, h_ref, xin_ref, idx_ref, zeros_ref, idx_sem,
                     row_sem, zero_sem):
    i = pl.program_id(0)
    n_tiles = pl.num_programs(0)
    t = h_ref.shape[0] // ROW_SUB
    per_tile = t * TOP_K
    slot = i % 2
    tm = zeros_ref.shape[0] // ROW_SUB
    n_blocks = xin_ref.shape[0] // zeros_ref.shape[0]

    def zero_copy(start):
        start = pl.multiple_of(start * ROW_SUB, ROW_SUB)
        return pltpu.make_async_copy(zeros_ref, xin_ref.at[pl.ds(start, tm * ROW_SUB)], zero_sem)

    def unused_block(op):
        def body(blk, carry):
            op(zero_copy(blk * tm))
            return carry
        lax.fori_loop(n_used_ref[0], n_blocks, body, 0)

    @pl.when(i == 0)
    def _():
        zeros_ref[...] = jnp.zeros_like(zeros_ref)
        _idx_copy(dest_ref, idx_ref, idx_sem, 0, 0, per_tile).start()
        for e in range(N_EXPERTS):
            @pl.when(tail_flag_ref[e] > 0)
            def _():
                zero_copy(tail_start_ref[e]).start()
        unused_block(lambda c: c.start())
        for e in range(N_EXPERTS):
            @pl.when(tail_flag_ref[e] > 0)
            def _():
                zero_copy(tail_start_ref[e]).wait()
        unused_block(lambda c: c.wait())

    @pl.when(i + 1 < n_tiles)
    def _():
        _idx_copy(dest_ref, idx_ref, idx_sem, i + 1, 1 - slot, per_tile).start()

    _idx_copy(dest_ref, idx_ref, idx_sem, i, slot, per_tile).wait()

    base = slot * per_tile

    def row_copy(r, k):
        return pltpu.make_async_copy(_row_tile(h_ref, r), _tile_at(xin_ref, idx_ref[base + r * TOP_K + k]), row_sem)

    def issue(r, carry):
        for k in range(TOP_K):
            row_copy(r, k).start(priority=k % 2)
        return carry

    def drain(r, carry):
        for k in range(TOP_K):
            row_copy(r, k).wait()
        return carry

    lax.fori_loop(0, t, issue, 0, unroll=8)
    lax.fori_loop(0, t, drain, 0, unroll=8)


def _dispatch(dest, h2t, tail_start, tail_flag, n_used, cap):
    n = h2t.shape[0] // ROW_SUB
    t = DISPATCH_T
    return pl.pallas_call(
        _dispatch_kernel,
        grid_spec=pltpu.PrefetchScalarGridSpec(
            num_scalar_prefetch=3,
            grid=(n // t,),
            in_specs=[pl.BlockSpec(memory_space=pl.ANY), pl.BlockSpec((t * ROW_SUB, LANES), lambda i, *_: (i, 0))],
            out_specs=pl.BlockSpec(memory_space=pl.ANY),
            scratch_shapes=[
                pltpu.SMEM((2 * t * TOP_K,), jnp.int32),
                pltpu.VMEM((EXPERT_TM * ROW_SUB, LANES), F32),
                pltpu.SemaphoreType.DMA((2,)),
                pltpu.SemaphoreType.DMA,
                pltpu.SemaphoreType.DMA,
            ],
        ),
        out_shape=jax.ShapeDtypeStruct((cap * ROW_SUB, LANES), F32),
        compiler_params=_params(("arbitrary",)),
        name="dispatch",
    )(tail_start, tail_flag, n_used, dest, h2t)


def _expert_kernel(blk_expert_ref, n_used_ref, x_ref, wgu_ref, bgu_ref, wd_ref, bd_ref, y_ref, wgu_bf, wd_bf):
    i = pl.program_id(0)
    f = wd_ref.shape[1]
    used = i < n_used_ref[0]

    new_expert = (i == 0) | (blk_expert_ref[i] != blk_expert_ref[jnp.maximum(i - 1, 0)])

    @pl.when(used & new_expert)
    def _():
        step = EXPERT_FC
        for r in range(0, wgu_ref.shape[1], step):
            wgu_bf[r:r + step, :] = wgu_ref[0, r:r + step, :].astype(BF16)
        for r in range(0, f, step):
            wd_bf[r:r + step, :] = wd_ref[0, r:r + step, :].astype(BF16)

    @pl.when(used)
    def _():
        x = _load_row_tiles(x_ref).astype(BF16)
        y = bd_ref[0]
        for c in range(f // EXPERT_FC):
            lo, hi = c * EXPERT_FC, (c + 1) * EXPERT_FC
            g = _dot(x, wgu_bf[:, lo:hi]) + bgu_ref[0, :, lo:hi]
            u = _dot(x, wgu_bf[:, f + lo:f + hi]) + bgu_ref[0, :, f + lo:f + hi]
            g = jnp.minimum(g, SWIGLU_LIMIT)
            u = jnp.clip(u, -SWIGLU_LIMIT, SWIGLU_LIMIT)
            a = (u + 1.0) * (g * _sigmoid(SWIGLU_ALPHA * g))
            y = y + _dot(a.astype(BF16), wd_bf[lo:hi, :])
        _store_row_tiles(y_ref, y)

    @pl.when(i >= n_used_ref[0])
    def _():
        y_ref[...] = jnp.zeros_like(y_ref)


def _experts(xin, blk_expert, n_used, w_gate_up, b_gate_up, w_down, b_down):
    cap = xin.shape[0] // ROW_SUB
    tm = EXPERT_TM
    ne, d, f2 = w_gate_up.shape
    f = f2 // 2
    rows = lambda i, be, nu: (jnp.minimum(i, nu[0] - 1), 0)
    by_expert = lambda i, be, nu: (be[i], 0, 0)
    return pl.pallas_call(
        _expert_kernel,
        grid_spec=pltpu.PrefetchScalarGridSpec(
            num_scalar_prefetch=2,
            grid=(cap // tm,),
            in_specs=[
                pl.BlockSpec((tm * ROW_SUB, LANES), rows),
                pl.BlockSpec((1, d, f2), by_expert),
                pl.BlockSpec((1, 1, f2), by_expert),
                pl.BlockSpec((1, f, d), by_expert),
                pl.BlockSpec((1, 1, d), by_expert),
            ],
            out_specs=pl.BlockSpec((tm * ROW_SUB, LANES), lambda i, be, nu: (i, 0)),
            scratch_shapes=[pltpu.VMEM((d, f2), BF16), pltpu.VMEM((f, d), BF16)],
        ),
        out_shape=jax.ShapeDtypeStruct((cap * ROW_SUB, LANES), F32),
        compiler_params=_params(("arbitrary",)),
        name="experts",
    )(blk_expert, n_used, xin, w_gate_up, b_gate_up.reshape(ne, 1, f2), w_down, b_down.reshape(ne, 1, d))


def _combine_kernel(dest_ref, y_ref, x1_ref, gw_ref, o_ref, idx_ref, rows_ref, idx_sem, row_sem):
    i = pl.program_id(0)
    n_tiles = pl.num_programs(0)
    t = x1_ref.shape[0]
    per_tile = t * TOP_K
    slot = i % 2

    def row_copy(tile_slot, r, k, d):
        dst = _row_tile(rows_ref, (tile_slot * TOP_K + k) * t + r)
        return pltpu.make_async_copy(_tile_at(y_ref, d), dst, row_sem.at[tile_slot])

    def gather(tile_slot):
        base = tile_slot * per_tile

        def issue(r, carry):
            for k in range(TOP_K):
                row_copy(tile_slot, r, k, idx_ref[base + r * TOP_K + k]).start(priority=k % 2)
            return carry

        lax.fori_loop(0, t, issue, 0, unroll=8)

    @pl.when(i == 0)
    def _():
        _idx_copy(dest_ref, idx_ref, idx_sem, 0, 0, per_tile).start()
        _idx_copy(dest_ref, idx_ref, idx_sem, 0, 0, per_tile).wait()
        gather(0)

        @pl.when(n_tiles > 1)
        def _():
            _idx_copy(dest_ref, idx_ref, idx_sem, 1, 1, per_tile).start()

    @pl.when(i + 1 < n_tiles)
    def _():
        _idx_copy(dest_ref, idx_ref, idx_sem, i + 1, 1 - slot, per_tile).wait()
        gather(1 - slot)

    @pl.when(i + 2 < n_tiles)
    def _():
        _idx_copy(dest_ref, idx_ref, idx_sem, i + 2, slot, per_tile).start()

    def drain(r, carry):
        for k in range(TOP_K):
            row_copy(slot, r, k, 0).wait()
        return carry

    lax.fori_loop(0, t, drain, 0, unroll=8)

    weights = [jnp.broadcast_to(gw_ref[:, k:k + 1], (t, LANES)) for k in range(TOP_K)]
    for j in range(ROW_SUB):
        acc = x1_ref[:, j * LANES:(j + 1) * LANES]
        for k in range(TOP_K):
            acc = acc + weights[k] * rows_ref[_group(rows_ref, j, t, base=(slot * TOP_K + k) * t)]
        o_ref[:, j * LANES:(j + 1) * LANES] = acc


def _combine(dest, yb, x1, gw):
    n, d = x1.shape
    t = COMBINE_T
    return pl.pallas_call(
        _combine_kernel,
        grid=(n // t,),
        in_specs=[
            pl.BlockSpec(memory_space=pl.ANY),
            pl.BlockSpec(memory_space=pl.ANY),
            pl.BlockSpec((t, d), lambda i: (i, 0)),
            pl.BlockSpec((t, LANES), lambda i: (i, 0)),
        ],
        out_specs=pl.BlockSpec((t, d), lambda i: (i, 0)),
        out_shape=jax.ShapeDtypeStruct((n, d), F32),
        scratch_shapes=[
            pltpu.SMEM((2 * t * TOP_K,), jnp.int32),
            pltpu.VMEM((2 * TOP_K * t * ROW_SUB, LANES), F32),
            pltpu.SemaphoreType.DMA((2,)),
            pltpu.SemaphoreType.DMA((2,)),
        ],
        compiler_params=_params(("arbitrary",)),
        name="combine",
    )(dest, yb, x1, gw)


def _layer(x, positions, norm1_g, w_in, b_forget, b_gate, q_norm_swa, k_norm_swa, sinks, q_norm_fox, k_norm_fox,
           w_o_swa, w_o_fox, w_out, norm2_g, w_router, b_router, w_gate_up, b_gate_up, w_down, b_down):
    b, s, d = x.shape
    n = b * s
    assert s % max(PROJ_TM, SWA_TQ, FOX_TQ) == 0 and n % max(MERGE_TM, DISPATCH_T, COMBINE_T) == 0
    assert d == ROW_SUB * LANES
    x2 = x.reshape(n, d)

    qa, ka2, va2, qf, kf, vf, gates = _in_proj(x2, positions.reshape(n, 1), norm1_g, w_in, b_forget, b_gate,
                                               q_norm_swa, k_norm_swa, q_norm_fox, k_norm_fox, s)
    seq = lambda a: a.reshape(b, s, a.shape[1])
    out_a = _swa(seq(qa), seq(ka2), seq(va2), sinks).reshape(n, SWA_Q_W)
    out_b = _fox(seq(qf), seq(kf), seq(vf)).reshape(n, FOX_W)

    x1, h2t, idx, gw, rank, totals = _merge(out_a, out_b, gates, x2, w_o_swa, w_o_fox, w_out, norm2_g, w_router,
                                            b_router)

    tm = EXPERT_TM
    cap = n * TOP_K + N_EXPERTS * tm
    counts = totals[0, :N_EXPERTS].astype(jnp.int32)
    padded = (counts + tm - 1) // tm * tm
    pad_end = jnp.cumsum(padded)
    pad_start = pad_end - padded
    dest = ((pad_start[idx[:, :TOP_K]] + rank[:, :TOP_K]) * ROW_SUB).reshape(n * TOP_K)
    blk_start = jnp.arange(cap // tm, dtype=jnp.int32) * tm
    blk_expert = jnp.minimum(jnp.sum(pad_end[None, :] <= blk_start[:, None], axis=1), N_EXPERTS - 1).astype(jnp.int32)
    n_used = (pad_end[-1:] // tm).astype(jnp.int32)

    xin = _dispatch(dest, h2t, (pad_end - tm).astype(jnp.int32), (padded > 0).astype(jnp.int32), n_used, cap)
    yb = _experts(xin, blk_expert, n_used, w_gate_up, b_gate_up, w_down, b_down)
    return _combine(dest, yb, x1, gw).reshape(b, s, d)


def kernel(x, positions, norm1_g, w_in, b_forget, b_gate, q_norm_swa, k_norm_swa, sinks, q_norm_fox, k_norm_fox,
           w_o_swa, w_o_fox, w_out, norm2_g, w_router, b_router, w_gate_up, b_gate_up, w_down, b_down):
    for l in range(norm1_g.shape[0]):
        x = _layer(x, positions, norm1_g[l], w_in[l], b_forget[l], b_gate[l], q_norm_swa[l], k_norm_swa[l],
                   sinks[l], q_norm_fox[l], k_norm_fox[l], w_o_swa[l], w_o_fox[l], w_out[l], norm2_g[l],
                   w_router[l], b_router[l], w_gate_up[l], b_gate_up[l], w_down[l], b_down[l])
    return x
```

```python
import functools

import jax
import jax.numpy as jnp
import numpy as np
from jax import lax
from jax.experimental import pallas as pl
from jax.experimental.pallas import tpu as pltpu

HEAD_DIM = 64
SWA_Q_HEADS = 8
SWA_KV_HEADS = 2
SWA_BLOCK = 128
FOX_HEADS = 8
ROPE_THETA = 10000.0
N_EXPERTS = 32
TOP_K = 4
SWIGLU_LIMIT = 7.0
SWIGLU_ALPHA = 1.702
EPS = 1e-6
NEG = -1e30
ATTN_SCALE = HEAD_DIM ** -0.5
LOG2E = 1.4426950408889634

LANES = 128
VMEM_LIMIT_BYTES = 56 * 1024 * 1024

SWA_Q_W = SWA_Q_HEADS * HEAD_DIM
SWA_KV_W = SWA_KV_HEADS * HEAD_DIM
FOX_W = FOX_HEADS * HEAD_DIM
FOX_AUG_W = FOX_HEADS * LANES

PROJ_TM = 1024
SWA_TQ = 512
FOX_TQ = 1024
FOX_SUB = 512
FOX_TK = 512
MERGE_TM = 512
DISPATCH_T = 256
EXPERT_TM = 512
EXPERT_FC = 1024
COMBINE_T = 256

BF16 = jnp.bfloat16
F32 = jnp.float32


def _params(sem):
    return pltpu.CompilerParams(dimension_semantics=sem, vmem_limit_bytes=VMEM_LIMIT_BYTES)


def _dot(a, b):
    return jnp.dot(a, b, preferred_element_type=F32)


def _dot_nt(a, b):
    return lax.dot_general(a, b, (((1,), (1,)), ((), ())), preferred_element_type=F32)


def _split3(v):
    hi = v.astype(BF16)
    r1 = v - hi.astype(F32)
    mid = r1.astype(BF16)
    lo = (r1 - mid.astype(F32)).astype(BF16)
    return hi, mid, lo


def _sigmoid(z):
    return 0.5 * jnp.tanh(0.5 * z) + 0.5


def _pack3(v):
    hi, mid, lo = _split3(v)
    packed = hi.astype(F32) + pltpu.roll(mid.astype(F32), FOX_HEADS, 1) + pltpu.roll(lo.astype(F32), 2 * FOX_HEADS, 1)
    return packed.astype(BF16)


def _lane_iota(shape):
    return lax.broadcasted_iota(jnp.int32, shape, len(shape) - 1)


def _head_norm(t, gain, bd):
    ss = _dot((t * t).astype(BF16), bd)
    return t * lax.rsqrt(ss * (1.0 / HEAD_DIM) + EPS) * gain


def _rope(t, cos, sin_signed, first_half):
    partner = jnp.where(first_half, pltpu.roll(t, LANES - HEAD_DIM // 2, 1), pltpu.roll(t, HEAD_DIM // 2, 1))
    return t * cos + partner * sin_signed


def _in_proj_kernel(tiles_per_batch, x_ref, pos_ref, g1_ref, wqkv_ref, wfl_ref, wgl_ref, bfl_ref, bgl_ref,
                    gqa_ref, gka_ref, gqf_ref, gkf_ref, invf_ref, eq_ref, ek_ref, oneq_ref, onek_ref,
                    qa_ref, ka_ref, va_ref, qf_ref, kf_ref, vf_ref, gate_ref, carry_ref):
    i = pl.program_id(0)
    tm = x_ref.shape[0]

    x = x_ref[...]
    h = x * lax.rsqrt(jnp.mean(x * x, axis=-1, keepdims=True) + EPS) * g1_ref[...]
    h = h.astype(BF16)

    lane = _lane_iota((tm, LANES))
    in_low_head = lane < HEAD_DIM
    first_half = (lane & (HEAD_DIM - 1)) < (HEAD_DIM // 2)
    shift = HEAD_DIM.bit_length() - 1
    rr = lax.broadcasted_iota(jnp.int32, (2 * LANES, 2 * LANES), 0) >> shift
    cc = lax.broadcasted_iota(jnp.int32, (2 * LANES, 2 * LANES), 1) >> shift
    bd2 = (rr == cc).astype(BF16)
    bd = bd2[:LANES, :LANES]

    ang = pos_ref[...].astype(F32) * invf_ref[...]
    cos = jnp.cos(ang)
    sin_signed = jnp.where(first_half, -jnp.sin(ang), jnp.sin(ang))

    fl = _dot(h, wfl_ref[...]) + bfl_ref[...]
    logf = jnp.minimum(fl, 0.0) - jnp.log(1.0 + jnp.exp(-jnp.abs(fl)))
    logf = jnp.where(lane < FOX_HEADS, logf, 0.0)
    row = lax.broadcasted_iota(jnp.int32, (tm, tm), 0)
    col = lax.broadcasted_iota(jnp.int32, (tm, tm), 1)
    tri = (col <= row).astype(BF16)
    c3 = _dot(tri, _pack3(logf))
    csum = c3 + pltpu.roll(c3, LANES - FOX_HEADS, 1) + pltpu.roll(c3, LANES - 2 * FOX_HEADS, 1)
    csum = jnp.where(lane < FOX_HEADS, csum, 0.0)

    @pl.when(i % tiles_per_batch == 0)
    def _():
        carry_ref[...] = jnp.zeros_like(carry_ref)

    csum = csum + carry_ref[0:1, :]
    carry_ref[0:1, :] = csum[tm - 1:tm, :]
    c_parts = _pack3(csum * LOG2E)

    wide = 2 * LANES
    for c in range(SWA_Q_W // wide):
        t = _dot(h, wqkv_ref[:, c * wide:(c + 1) * wide])
        t = _head_norm(t, gqa_ref[:, c * wide:(c + 1) * wide], bd2)
        for g in range(2):
            lo = c * wide + g * LANES
            qa_ref[:, lo:lo + LANES] = _rope(t[:, g * LANES:(g + 1) * LANES], cos, sin_signed, first_half).astype(BF16)
    off = SWA_Q_W
    t = _dot(h, wqkv_ref[:, off:off + LANES])
    t = _rope(_head_norm(t, gka_ref[...], bd), cos, sin_signed, first_half)
    ka_ref[:, 0:LANES] = t.astype(BF16)
    ka_ref[:, LANES:2 * LANES] = pltpu.roll(t, HEAD_DIM, 1).astype(BF16)
    off += SWA_KV_W
    t = _dot(h, wqkv_ref[:, off:off + LANES])
    va_ref[:, 0:LANES] = t.astype(BF16)
    va_ref[:, LANES:2 * LANES] = pltpu.roll(t, HEAD_DIM, 1).astype(BF16)
    off += SWA_KV_W

    aug_q = _dot(c_parts, eq_ref[...]) + oneq_ref[...]
    aug_k = _dot(c_parts, ek_ref[...]) + onek_ref[...]
    for name_off, gain_ref, aug, out_ref in ((off, gqf_ref, aug_q, qf_ref), (off + FOX_W, gkf_ref, aug_k, kf_ref)):
        for c in range(FOX_W // wide):
            t2 = _dot(h, wqkv_ref[:, name_off + c * wide:name_off + (c + 1) * wide])
            t2 = _head_norm(t2, gain_ref[:, c * wide:(c + 1) * wide], bd2)
            for g in range(2):
                t = t2[:, g * LANES:(g + 1) * LANES]
                even = 2 * (2 * c + g) * LANES
                odd = even + LANES
                out_ref[:, even:even + LANES] = jnp.where(in_low_head, t, aug[:, even:even + LANES]).astype(BF16)
                out_ref[:, odd:odd + LANES] = jnp.where(in_low_head, aug[:, odd:odd + LANES], t).astype(BF16)
    off += 2 * FOX_W
    for c in range(FOX_W // LANES):
        t = _dot(h, wqkv_ref[:, off + c * LANES:off + (c + 1) * LANES])
        vf_ref[:, 2 * c * LANES:(2 * c + 1) * LANES] = jnp.where(in_low_head, t, 1.0).astype(BF16)
        vf_ref[:, (2 * c + 1) * LANES:(2 * c + 2) * LANES] = jnp.where(in_low_head, 1.0, t).astype(BF16)

    gw = 512
    for c in range(gate_ref.shape[1] // gw):
        z = _dot(h, wgl_ref[:, c * gw:(c + 1) * gw]) + bgl_ref[:, c * gw:(c + 1) * gw]
        gate_ref[:, c * gw:(c + 1) * gw] = _sigmoid(z).astype(BF16)


def _aug_placement():
    eq = np.zeros((LANES, FOX_AUG_W), np.float32)
    ek = np.zeros((LANES, FOX_AUG_W), np.float32)
    oneq = np.zeros((1, FOX_AUG_W), np.float32)
    onek = np.zeros((1, FOX_AUG_W), np.float32)
    for h in range(FOX_HEADS):
        base = h * LANES + (HEAD_DIM if h % 2 == 0 else 0)
        for p in range(3):
            eq[p * FOX_HEADS + h, base + p] = 1.0
            ek[p * FOX_HEADS + h, base + 3 + p] = -1.0
            oneq[0, base + 3 + p] = 1.0
            onek[0, base + p] = 1.0
    return jnp.asarray(eq, BF16), jnp.asarray(ek, BF16), jnp.asarray(oneq), jnp.asarray(onek)


def _in_proj(x2, pos2, norm1_g, w_in, b_forget, b_gate, q_norm_swa, k_norm_swa, q_norm_fox, k_norm_fox, seq):
    n, d = x2.shape
    tm = PROJ_TM
    n_qkv = SWA_Q_W + 2 * SWA_KV_W + 3 * FOX_W
    wqkv = w_in[:, :n_qkv].astype(BF16)
    wfl = jnp.pad(w_in[:, n_qkv:n_qkv + FOX_HEADS], ((0, 0), (0, LANES - FOX_HEADS))).astype(BF16)
    wgl = w_in[:, n_qkv + FOX_HEADS:].astype(BF16)
    n_gate = wgl.shape[1]
    bfl = jnp.pad(b_forget, (0, LANES - FOX_HEADS)).reshape(1, LANES)
    bgl = b_gate.reshape(1, n_gate)
    gqa = (jnp.tile(q_norm_swa, SWA_Q_HEADS) * ATTN_SCALE).reshape(1, SWA_Q_W)
    gka = jnp.tile(k_norm_swa, SWA_KV_HEADS).reshape(1, SWA_KV_W)
    gqf = (jnp.tile(q_norm_fox, FOX_HEADS) * (ATTN_SCALE * LOG2E)).reshape(1, FOX_W)
    gkf = jnp.tile(k_norm_fox, FOX_HEADS).reshape(1, FOX_W)
    half = HEAD_DIM // 2
    inv = ROPE_THETA ** (-jnp.arange(half, dtype=F32) / half)
    invf = jnp.tile(inv, LANES // half).reshape(1, LANES)
    eq, ek, oneq, onek = _aug_placement()

    row = lambda w: pl.BlockSpec((tm, w), lambda i: (i, 0))
    full = lambda a: pl.BlockSpec(a.shape, lambda i: (0,) * a.ndim, pipeline_mode=pl.Buffered(1))
    consts = (norm1_g.reshape(1, d), wqkv, wfl, wgl, bfl, bgl, gqa, gka, gqf, gkf, invf, eq, ek, oneq, onek)
    out_widths = (SWA_Q_W, 2 * SWA_KV_W, 2 * SWA_KV_W, FOX_AUG_W, FOX_AUG_W, 2 * FOX_W, n_gate)
    return pl.pallas_call(
        functools.partial(_in_proj_kernel, seq // tm),
        grid=(n // tm,),
        in_specs=[row(d), row(1)] + [full(a) for a in consts],
        out_specs=[row(w) for w in out_widths],
        out_shape=[jax.ShapeDtypeStruct((n, w), BF16) for w in out_widths],
        scratch_shapes=[pltpu.VMEM((8, LANES), F32)],
        compiler_params=_params(("arbitrary",)),
        name="in_proj",
    )(x2, pos2, *consts)


def _swa_kernel(sinks_ref, q_ref, kc_ref, kp_ref, vc_ref, vp_ref, o_ref):
    i = pl.program_id(1)
    tq = q_ref.shape[1]
    blk = SWA_BLOCK
    lane = _lane_iota((2 * blk, LANES))
    low = lane < HEAD_DIM
    qi = lax.broadcasted_iota(jnp.int32, (blk, 2 * blk), 0)
    kj = lax.broadcasted_iota(jnp.int32, (blk, 2 * blk), 1)
    cur_ok = (kj >= blk) & (kj - blk <= qi)
    prev_ok = (kj < blk) & (kj > qi)
    zero = jnp.zeros((2 * blk, LANES), BF16)
    group = SWA_Q_HEADS // SWA_KV_HEADS

    for j in range(tq // blk):
        if j == 0:
            k_prev, v_prev = kp_ref[0], vp_ref[0]
            valid = cur_ok | (prev_ok & (i > 0))
        else:
            k_prev, v_prev = kc_ref[0, (j - 1) * blk:j * blk, :], vc_ref[0, (j - 1) * blk:j * blk, :]
            valid = cur_ok | prev_ok
        kwin = jnp.concatenate([k_prev, kc_ref[0, j * blk:(j + 1) * blk, :]], axis=0)
        vwin = jnp.concatenate([v_prev, vc_ref[0, j * blk:(j + 1) * blk, :]], axis=0)
        nat_k, swp_k = kwin[:, :LANES], kwin[:, LANES:]
        nat_v, swp_v = vwin[:, :LANES], vwin[:, LANES:]
        k_var = ((jnp.where(low, nat_k, zero), jnp.where(low, zero, swp_k)),
                 (jnp.where(low, swp_k, zero), jnp.where(low, zero, nat_k)))
        v_var = ((jnp.where(low, nat_v, zero), jnp.where(low, zero, swp_v)),
                 (jnp.where(low, swp_v, zero), jnp.where(low, zero, nat_v)))
        for c in range(SWA_Q_W // LANES):
            qc = q_ref[0, j * blk:(j + 1) * blk, c * LANES:(c + 1) * LANES]
            out = jnp.zeros((blk, LANES), F32)
            for half in range(2):
                head = 2 * c + half
                kv = head // group
                s = jnp.where(valid, _dot_nt(qc, k_var[kv][half]), NEG)
                sink = sinks_ref[head]
                m = jnp.maximum(jnp.max(s, axis=-1, keepdims=True), sink)
                p = jnp.exp(s - m)
                den = jnp.sum(p, axis=-1, keepdims=True) + jnp.exp(sink - m)
                out = out + _dot(p.astype(BF16), v_var[kv][half]) / den
            o_ref[0, j * blk:(j + 1) * blk, c * LANES:(c + 1) * LANES] = out.astype(BF16)


def _swa(qa, ka2, va2, sinks):
    b, s, _ = qa.shape
    tq = SWA_TQ
    per = tq // SWA_BLOCK
    cur = lambda w: pl.BlockSpec((1, tq, w), lambda bi, i, sk: (bi, i, 0))
    prev = lambda w: pl.BlockSpec((1, SWA_BLOCK, w), lambda bi, i, sk: (bi, jnp.maximum(i * per - 1, 0), 0))
    return pl.pallas_call(
        _swa_kernel,
        grid_spec=pltpu.PrefetchScalarGridSpec(
            num_scalar_prefetch=1,
            grid=(b, s // tq),
            in_specs=[cur(SWA_Q_W), cur(2 * SWA_KV_W), prev(2 * SWA_KV_W), cur(2 * SWA_KV_W), prev(2 * SWA_KV_W)],
            out_specs=cur(SWA_Q_W),
        ),
        out_shape=jax.ShapeDtypeStruct((b, s, SWA_Q_W), BF16),
        compiler_params=_params(("arbitrary", "arbitrary")),
        name="swa",
    )(sinks, qa, ka2, ka2, va2, va2)


def _fox_kernel(q_ref, k_ref, v_ref, o_ref, m_ref, acc_ref, s_ref):
    qi = pl.program_id(2)
    sub, tk = FOX_SUB, FOX_TK
    tq = q_ref.shape[1]
    n_sub = tq // sub
    n_chunk = tk // LANES
    row = lax.broadcasted_iota(jnp.int32, (sub, LANES), 0)
    col = lax.broadcasted_iota(jnp.int32, (sub, LANES), 1)
    chains = [(half, sb) for sb in range(n_sub) for half in range(2)]

    m_ref[...] = jnp.full_like(m_ref, NEG)
    acc_ref[...] = jnp.zeros_like(acc_ref)

    def scores(buf, chain, kb):
        half, sb = chains[chain]
        start = pl.multiple_of(kb * tk, tk)
        q = q_ref[0, sb * sub:(sb + 1) * sub, half * LANES:(half + 1) * LANES]
        k = k_ref[0, pl.ds(start, tk), half * LANES:(half + 1) * LANES]
        s_ref[buf, chain] = _dot_nt(q, k)

    def fold(buf, chain, kb, mask_offset):
        half, sb = chains[chain]
        start = pl.multiple_of(kb * tk, tk)
        v = v_ref[0, pl.ds(start, tk), half * LANES:(half + 1) * LANES]
        s = [s_ref[buf, chain, :, c * LANES:(c + 1) * LANES] for c in range(n_chunk)]
        if mask_offset is not None:
            s = [jnp.where(col + (c * LANES + mask_offset) <= row, s[c], NEG) for c in range(n_chunk)]
        m_prev = m_ref[chain]
        peak = functools.reduce(jnp.maximum, s)
        m_new = jnp.maximum(m_prev, jnp.max(peak, axis=-1, keepdims=True))
        alpha = jnp.exp2(m_prev - m_new)
        p = [jnp.exp2(sc - m_new) for sc in s]
        pv = _dot(jnp.concatenate(p, axis=1).astype(BF16), v)
        acc_ref[chain] = alpha * acc_ref[chain] + pv
        m_ref[chain] = m_new

    assert tq // tk == 2
    every = range(len(chains))
    for chain in every:
        scores(0, chain, 0)

    def body(t, carry):
        for chain in every:
            scores(1, chain, 2 * t + 1)
        for chain in every:
            fold(0, chain, 2 * t, None)
        for chain in every:
            scores(0, chain, 2 * t + 2)
        for chain in every:
            fold(1, chain, 2 * t + 1, None)
        return carry

    lax.fori_loop(0, qi, body, 0)

    def visibility(j, sb):
        first_key, first_row = j * tk, sb * sub
        if first_key + tk - 1 <= first_row:
            return True, None
        if first_key <= first_row + sub - 1:
            return True, first_key - first_row
        return False, None

    for chain, (half, sb) in enumerate(chains):
        if visibility(1, sb)[0]:
            scores(1, chain, 2 * qi + 1)
    for j in range(2):
        for chain, (half, sb) in enumerate(chains):
            visible, mask_offset = visibility(j, sb)
            if visible:
                fold(j, chain, 2 * qi + j, mask_offset)

    low = _lane_iota((sub, LANES)) < HEAD_DIM
    for sb in range(n_sub):
        even, odd = acc_ref[2 * sb], acc_ref[2 * sb + 1]
        even = even / pltpu.roll(even, HEAD_DIM, 1)
        odd = odd / pltpu.roll(odd, HEAD_DIM, 1)
        o_ref[0, sb * sub:(sb + 1) * sub, :] = jnp.where(low, even, odd).astype(BF16)


def _fox(qf, kf, vf):
    b, s, _ = qf.shape
    tq = FOX_TQ
    pairs = FOX_HEADS // 2
    n_chain = 2 * (tq // FOX_SUB)
    return pl.pallas_call(
        _fox_kernel,
        grid=(b, pairs, s // tq),
        in_specs=[
            pl.BlockSpec((1, tq, 2 * LANES), lambda bi, p, i: (bi, i, p)),
            pl.BlockSpec((1, s, 2 * LANES), lambda bi, p, i: (bi, 0, p)),
            pl.BlockSpec((1, s, 2 * LANES), lambda bi, p, i: (bi, 0, p)),
        ],
        out_specs=pl.BlockSpec((1, tq, LANES), lambda bi, p, i: (bi, i, p)),
        out_shape=jax.ShapeDtypeStruct((b, s, FOX_W), BF16),
        scratch_shapes=[pltpu.VMEM((n_chain, FOX_SUB, LANES), F32)] * 2
        + [pltpu.VMEM((2, n_chain, FOX_SUB, FOX_TK), F32)],
        compiler_params=_params(("arbitrary", "arbitrary", "arbitrary")),
        name="fox",
    )(qf, kf, vf)


ROW_SUB = 8


def _group(ref, j, rows, base=0):
    return (pl.ds(base * ROW_SUB + j, rows, stride=ROW_SUB), slice(None))


def _store_row_tiles(ref, v):
    for j in range(ROW_SUB):
        ref[_group(ref, j, v.shape[0])] = v[:, j * LANES:(j + 1) * LANES]


def _load_row_tiles(ref):
    rows = ref.shape[0] // ROW_SUB
    return jnp.concatenate([ref[_group(ref, j, rows)] for j in range(ROW_SUB)], axis=1)


def _tile_at(ref, first_sublane):
    return ref.at[pl.ds(pl.multiple_of(first_sublane, ROW_SUB), ROW_SUB)]


def _row_tile(ref, r):
    return _tile_at(ref, r * ROW_SUB)


def _merge_kernel(oa_ref, ob_ref, gate_ref, x_ref, woa_ref, wob_ref, wout_ref, g2_ref, wr2_ref, br_ref,
                  x1_ref, h2_ref, idx_ref, gw_ref, rank_ref, cnt_ref, carry_ref):
    i = pl.program_id(0)
    tm, d = x_ref.shape

    ga = gate_ref[:, :d].astype(F32)
    gb = gate_ref[:, d:].astype(F32)
    merged = ga * _dot(oa_ref[...], woa_ref[...]) + gb * _dot(ob_ref[...], wob_ref[...])
    x1 = x_ref[...] + _dot(merged.astype(BF16), wout_ref[...])
    x1_ref[...] = x1
    h2 = x1 * lax.rsqrt(jnp.mean(x1 * x1, axis=-1, keepdims=True) + EPS) * g2_ref[...]
    _store_row_tiles(h2_ref, h2)

    h_hi = h2.astype(BF16)
    h_lo = (h2 - h_hi.astype(F32)).astype(BF16)
    both = _dot(h_hi, wr2_ref[...])
    logits = both[:, :LANES] + both[:, LANES:] + _dot(h_lo, wr2_ref[:, :LANES]) + br_ref[...]
    lane = _lane_iota((tm, LANES))
    lanef = lane.astype(F32)
    logits = jnp.where(lane < N_EXPERTS, logits, NEG)

    sels, vals = [], []
    for _ in range(TOP_K):
        mx = jnp.max(logits, axis=-1, keepdims=True)
        ix = jnp.min(jnp.where(logits == mx, lanef, float(LANES)), axis=-1, keepdims=True)
        sel = lanef == ix
        sels.append((sel, ix))
        vals.append(mx)
        logits = jnp.where(sel, 3.0 * NEG, logits)
    exps = [jnp.exp(v - vals[0]) for v in vals]
    den = exps[0] + exps[1] + exps[2] + exps[3]

    @pl.when(i == 0)
    def _():
        carry_ref[...] = jnp.zeros_like(carry_ref)

    onehot = jnp.zeros((tm, LANES), F32)
    for sel, _ in sels:
        onehot = onehot + sel.astype(F32)
    row = lax.broadcasted_iota(jnp.int32, (tm, tm), 0)
    col = lax.broadcasted_iota(jnp.int32, (tm, tm), 1)
    before = _dot((col < row).astype(BF16), onehot.astype(BF16)) + carry_ref[0:1, :]
    total = carry_ref[0:1, :] + jnp.sum(onehot, axis=0, keepdims=True)
    carry_ref[0:1, :] = total
    cnt_ref[...] = jnp.broadcast_to(total, cnt_ref.shape)

    idx_out = jnp.zeros((tm, LANES), F32)
    gw_out = jnp.zeros((tm, LANES), F32)
    rank_out = jnp.zeros((tm, LANES), F32)
    for k, (sel, ix) in enumerate(sels):
        slot = lane == k
        rank = jnp.sum(jnp.where(sel, before, 0.0), axis=-1, keepdims=True)
        idx_out = jnp.where(slot, ix, idx_out)
        gw_out = jnp.where(slot, exps[k] / den, gw_out)
        rank_out = jnp.where(slot, rank, rank_out)
    idx_ref[...] = idx_out.astype(jnp.int32)
    gw_ref[...] = gw_out
    rank_ref[...] = rank_out.astype(jnp.int32)


def _merge(out_a, out_b, gates, x2, w_o_swa, w_o_fox, w_out, norm2_g, w_router, b_router):
    n, d = x2.shape
    tm = MERGE_TM
    wr = jnp.pad(w_router, ((0, 0), (0, LANES - N_EXPERTS)))
    wrh = wr.astype(BF16)
    wrl = (wr - wrh.astype(F32)).astype(BF16)
    wr2 = jnp.concatenate([wrh, wrl], axis=1)
    br = jnp.pad(b_router, (0, LANES - N_EXPERTS)).reshape(1, LANES)
    consts = (w_o_swa.astype(BF16), w_o_fox.astype(BF16), w_out.astype(BF16), norm2_g.reshape(1, d), wr2, br)
    row = lambda w: pl.BlockSpec((tm, w), lambda i: (i, 0))
    full = lambda a: pl.BlockSpec(a.shape, lambda i: (0,) * a.ndim)
    return pl.pallas_call(
        _merge_kernel,
        grid=(n // tm,),
        in_specs=[row(out_a.shape[1]), row(out_b.shape[1]), row(gates.shape[1]), row(d)] + [full(a) for a in consts],
        out_specs=[row(d), pl.BlockSpec((tm * ROW_SUB, LANES), lambda i: (i, 0)), row(LANES), row(LANES), row(LANES),
                   pl.BlockSpec((8, LANES), lambda i: (0, 0))],
        out_shape=[
            jax.ShapeDtypeStruct((n, d), F32),
            jax.ShapeDtypeStruct((n * ROW_SUB, LANES), F32),
            jax.ShapeDtypeStruct((n, LANES), jnp.int32),
            jax.ShapeDtypeStruct((n, LANES), F32),
            jax.ShapeDtypeStruct((n, LANES), jnp.int32),
            jax.ShapeDtypeStruct((8, LANES), F32),
        ],
        scratch_shapes=[pltpu.VMEM((8, LANES), F32)],
        compiler_params=_params(("arbitrary",)),
        name="merge",
    )(out_a, out_b, gates, x2, *consts)


def _idx_copy(dest_ref, idx_ref, sem_ref, tile, slot, per_tile):
    src = dest_ref.at[pl.ds(pl.multiple_of(tile * per_tile, per_tile), per_tile)]
    dst = idx_ref.at[pl.ds(pl.multiple_of(slot * per_tile, per_tile), per_tile)]
    return pltpu.make_async_copy(src, dst, sem_ref.at[slot])


def _dispatch_kernel(tail_start_ref, tail_flag_ref, n_used_ref, dest_ref, h_ref, xin_ref, idx_ref, zeros_ref, idx_sem,
                     row_sem, zero_sem):
    i = pl.program_id(0)
    n_tiles = pl.num_programs(0)
    t = h_ref.shape[0] // ROW_SUB
    per_tile = t * TOP_K
    slot = i % 2
    tm = zeros_ref.shape[0] // ROW_SUB
    n_blocks = xin_ref.shape[0] // zeros_ref.shape[0]

    def zero_copy(start):
        start = pl.multiple_of(start * ROW_SUB, ROW_SUB)
        return pltpu.make_async_copy(zeros_ref, xin_ref.at[pl.ds(start, tm * ROW_SUB)], zero_sem)

    def unused_block(op):
        def body(blk, carry):
            op(zero_copy(blk * tm))
            return carry
        lax.fori_loop(n_used_ref[0], n_blocks, body, 0)

    @pl.when(i == 0)
    def _():
        zeros_ref[...] = jnp.zeros_like(zeros_ref)
        _idx_copy(dest_ref, idx_ref, idx_sem, 0, 0, per_tile).start()
        for e in range(N_EXPERTS):
            @pl.when(tail_flag_ref[e] > 0)
            def _():
                zero_copy(tail_start_ref[e]).start()
        unused_block(lambda c: c.start())
        for e in range(N_EXPERTS):
            @pl.when(tail_flag_ref[e] > 0)
            def _():
                zero_copy(tail_start_ref[e]).wait()
        unused_block(lambda c: c.wait())

    @pl.when(i + 1 < n_tiles)
    def _():
        _idx_copy(dest_ref, idx_ref, idx_sem, i + 1, 1 - slot, per_tile).start()

    _idx_copy(dest_ref, idx_ref, idx_sem, i, slot, per_tile).wait()

    base = slot * per_tile

    def row_copy(r, k):
        return pltpu.make_async_copy(_row_tile(h_ref, r), _tile_at(xin_ref, idx_ref[base + r * TOP_K + k]), row_sem)

    def issue(r, carry):
        for k in range(TOP_K):
            row_copy(r, k).start(priority=k % 2)
        return carry

    def drain(r, carry):
        for k in range(TOP_K):
            row_copy(r, k).wait()
        return carry

    lax.fori_loop(0, t, issue, 0, unroll=8)
    lax.fori_loop(0, t, drain, 0, unroll=8)


def _dispatch(dest, h2t, tail_start, tail_flag, n_used, cap):
    n = h2t.shape[0] // ROW_SUB
    t = DISPATCH_T
    return pl.pallas_call(
        _dispatch_kernel,
        grid_spec=pltpu.PrefetchScalarGridSpec(
            num_scalar_prefetch=3,
            grid=(n // t,),
            in_specs=[pl.BlockSpec(memory_space=pl.ANY), pl.BlockSpec((t * ROW_SUB, LANES), lambda i, *_: (i, 0))],
            out_specs=pl.BlockSpec(memory_space=pl.ANY),
            scratch_shapes=[
                pltpu.SMEM((2 * t * TOP_K,), jnp.int32),
                pltpu.VMEM((EXPERT_TM * ROW_SUB, LANES), F32),
                pltpu.SemaphoreType.DMA((2,)),
                pltpu.SemaphoreType.DMA,
                pltpu.SemaphoreType.DMA,
            ],
        ),
        out_shape=jax.ShapeDtypeStruct((cap * ROW_SUB, LANES), F32),
        compiler_params=_params(("arbitrary",)),
        name="dispatch",
    )(tail_start, tail_flag, n_used, dest, h2t)


def _expert_kernel(blk_expert_ref, n_used_ref, x_ref, wgu_ref, bgu_ref, wd_ref, bd_ref, y_ref, wgu_bf, wd_bf):
    i = pl.program_id(0)
    f = wd_ref.shape[1]
    used = i < n_used_ref[0]

    new_expert = (i == 0) | (blk_expert_ref[i] != blk_expert_ref[jnp.maximum(i - 1, 0)])

    @pl.when(used & new_expert)
    def _():
        step = EXPERT_FC
        for r in range(0, wgu_ref.shape[1], step):
            wgu_bf[r:r + step, :] = wgu_ref[0, r:r + step, :].astype(BF16)
        for r in range(0, f, step):
            wd_bf[r:r + step, :] = wd_ref[0, r:r + step, :].astype(BF16)

    @pl.when(used)
    def _():
        x = _load_row_tiles(x_ref).astype(BF16)
        y = bd_ref[0]
        for c in range(f // EXPERT_FC):
            lo, hi = c * EXPERT_FC, (c + 1) * EXPERT_FC
            g = _dot(x, wgu_bf[:, lo:hi]) + bgu_ref[0, :, lo:hi]
            u = _dot(x, wgu_bf[:, f + lo:f + hi]) + bgu_ref[0, :, f + lo:f + hi]
            g = jnp.minimum(g, SWIGLU_LIMIT)
            u = jnp.clip(u, -SWIGLU_LIMIT, SWIGLU_LIMIT)
            a = (u + 1.0) * (g * _sigmoid(SWIGLU_ALPHA * g))
            y = y + _dot(a.astype(BF16), wd_bf[lo:hi, :])
        _store_row_tiles(y_ref, y)

    @pl.when(i >= n_used_ref[0])
    def _():
        y_ref[...] = jnp.zeros_like(y_ref)


def _experts(xin, blk_expert, n_used, w_gate_up, b_gate_up, w_down, b_down):
    cap = xin.shape[0] // ROW_SUB
    tm = EXPERT_TM
    ne, d, f2 = w_gate_up.shape
    f = f2 // 2
    rows = lambda i, be, nu: (jnp.minimum(i, nu[0] - 1), 0)
    by_expert = lambda i, be, nu: (be[i], 0, 0)
    return pl.pallas_call(
        _expert_kernel,
        grid_spec=pltpu.PrefetchScalarGridSpec(
            num_scalar_prefetch=2,
            grid=(cap // tm,),
            in_specs=[
                pl.BlockSpec((tm * ROW_SUB, LANES), rows),
                pl.BlockSpec((1, d, f2), by_expert),
                pl.BlockSpec((1, 1, f2), by_expert),
                pl.BlockSpec((1, f, d), by_expert),
                pl.BlockSpec((1, 1, d), by_expert),
            ],
            out_specs=pl.BlockSpec((tm * ROW_SUB, LANES), lambda i, be, nu: (i, 0)),
            scratch_shapes=[pltpu.VMEM((d, f2), BF16), pltpu.VMEM((f, d), BF16)],
        ),
        out_shape=jax.ShapeDtypeStruct((cap * ROW_SUB, LANES), F32),
        compiler_params=_params(("arbitrary",)),
        name="experts",
    )(blk_expert, n_used, xin, w_gate_up, b_gate_up.reshape(ne, 1, f2), w_down, b_down.reshape(ne, 1, d))


def _combine_kernel(dest_ref, y_ref, x1_ref, gw_ref, o_ref, idx_ref, rows_ref, idx_sem, row_sem):
    i = pl.program_id(0)
    n_tiles = pl.num_programs(0)
    t = x1_ref.shape[0]
    per_tile = t * TOP_K
    slot = i % 2

    def row_copy(tile_slot, r, k, d):
        dst = _row_tile(rows_ref, (tile_slot * TOP_K + k) * t + r)
        return pltpu.make_async_copy(_tile_at(y_ref, d), dst, row_sem.at[tile_slot])

    def gather(tile_slot):
        base = tile_slot * per_tile

        def issue(r, carry):
            for k in range(TOP_K):
                row_copy(tile_slot, r, k, idx_ref[base + r * TOP_K + k]).start(priority=k % 2)
            return carry

        lax.fori_loop(0, t, issue, 0, unroll=8)

    @pl.when(i == 0)
    def _():
        _idx_copy(dest_ref, idx_ref, idx_sem, 0, 0, per_tile).start()
        _idx_copy(dest_ref, idx_ref, idx_sem, 0, 0, per_tile).wait()
        gather(0)

        @pl.when(n_tiles > 1)
        def _():
            _idx_copy(dest_ref, idx_ref, idx_sem, 1, 1, per_tile).start()

    @pl.when(i + 1 < n_tiles)
    def _():
        _idx_copy(dest_ref, idx_ref, idx_sem, i + 1, 1 - slot, per_tile).wait()
        gather(1 - slot)

    @pl.when(i + 2 < n_tiles)
    def _():
        _idx_copy(dest_ref, idx_ref, idx_sem, i + 2, slot, per_tile).start()

    def drain(r, carry):
        for k in range(TOP_K):
            row_copy(slot, r, k, 0).wait()
        return carry

    lax.fori_loop(0, t, drain, 0, unroll=8)

    weights = [jnp.broadcast_to(gw_ref[:, k:k + 1], (t, LANES)) for k in range(TOP_K)]
    for j in range(ROW_SUB):
        acc = x1_ref[:, j * LANES:(j + 1) * LANES]
        for k in range(TOP_K):
            acc = acc + weights[k] * rows_ref[_group(rows_ref, j, t, base=(slot * TOP_K + k) * t)]
        o_ref[:, j * LANES:(j + 1) * LANES] = acc


def _combine(dest, yb, x1, gw):
    n, d = x1.shape
    t = COMBINE_T
    return pl.pallas_call(
        _combine_kernel,
        grid=(n // t,),
        in_specs=[
            pl.BlockSpec(memory_space=pl.ANY),
            pl.BlockSpec(memory_space=pl.ANY),
            pl.BlockSpec((t, d), lambda i: (i, 0)),
            pl.BlockSpec((t, LANES), lambda i: (i, 0)),
        ],
        out_specs=pl.BlockSpec((t, d), lambda i: (i, 0)),
        out_shape=jax.ShapeDtypeStruct((n, d), F32),
        scratch_shapes=[
            pltpu.SMEM((2 * t * TOP_K,), jnp.int32),
            pltpu.VMEM((2 * TOP_K * t * ROW_SUB, LANES), F32),
            pltpu.SemaphoreType.DMA((2,)),
            pltpu.SemaphoreType.DMA((2,)),
        ],
        compiler_params=_params(("arbitrary",)),
        name="combine",
    )(dest, yb, x1, gw)


def _layer(x, positions, norm1_g, w_in, b_forget, b_gate, q_norm_swa, k_norm_swa, sinks, q_norm_fox, k_norm_fox,
           w_o_swa, w_o_fox, w_out, norm2_g, w_router, b_router, w_gate_up, b_gate_up, w_down, b_down):
    b, s, d = x.shape
    n = b * s
    assert s % max(PROJ_TM, SWA_TQ, FOX_TQ) == 0 and n % max(MERGE_TM, DISPATCH_T, COMBINE_T) == 0
    assert d == ROW_SUB * LANES
    x2 = x.reshape(n, d)

    qa, ka2, va2, qf, kf, vf, gates = _in_proj(x2, positions.reshape(n, 1), norm1_g, w_in, b_forget, b_gate,
                                               q_norm_swa, k_norm_swa, q_norm_fox, k_norm_fox, s)
    seq = lambda a: a.reshape(b, s, a.shape[1])
    out_a = _swa(seq(qa), seq(ka2), seq(va2), sinks).reshape(n, SWA_Q_W)
    out_b = _fox(seq(qf), seq(kf), seq(vf)).reshape(n, FOX_W)

    x1, h2t, idx, gw, rank, totals = _merge(out_a, out_b, gates, x2, w_o_swa, w_o_fox, w_out, norm2_g, w_router,
                                            b_router)

    tm = EXPERT_TM
    cap = n * TOP_K + N_EXPERTS * tm
    counts = totals[0, :N_EXPERTS].astype(jnp.int32)
    padded = (counts + tm - 1) // tm * tm
    pad_end = jnp.cumsum(padded)
    pad_start = pad_end - padded
    dest = ((pad_start[idx[:, :TOP_K]] + rank[:, :TOP_K]) * ROW_SUB).reshape(n * TOP_K)
    blk_start = jnp.arange(cap // tm, dtype=jnp.int32) * tm
    blk_expert = jnp.minimum(jnp.sum(pad_end[None, :] <= blk_start[:, None], axis=1), N_EXPERTS - 1).astype(jnp.int32)
    n_used = (pad_end[-1:] // tm).astype(jnp.int32)

    xin = _dispatch(dest, h2t, (pad_end - tm).astype(jnp.int32), (padded > 0).astype(jnp.int32), n_used, cap)
    yb = _experts(xin, blk_expert, n_used, w_gate_up, b_gate_up, w_down, b_down)
    return _combine(dest, yb, x1, gw).reshape(b, s, d)


def kernel(x, positions, norm1_g, w_in, b_forget, b_gate, q_norm_swa, k_norm_swa, sinks, q_norm_fox, k_norm_fox,
           w_o_swa, w_o_fox, w_out, norm2_g, w_router, b_router, w_gate_up, b_gate_up, w_down, b_down):
    for l in range(norm1_g.shape[0]):
        x = _layer(x, positions, norm1_g[l], w_in[l], b_forget[l], b_gate[l], q_norm_swa[l], k_norm_swa[l],
                   sinks[l], q_norm_fox[l], k_norm_fox[l], w_o_swa[l], w_o_fox[l], w_out[l], norm2_g[l],
                   w_router[l], b_router[l], w_gate_up[l], b_gate_up[l], w_down[l], b_down[l])
    return x
```

```python
import functools

import jax
import jax.numpy as jnp
import numpy as np
from jax import lax
from jax.experimental import pallas as pl
from jax.experimental.pallas import tpu as pltpu

HEAD_DIM = 64
SWA_Q_HEADS = 8
SWA_KV_HEADS = 2
SWA_BLOCK = 128
FOX_HEADS = 8
ROPE_THETA = 10000.0
N_EXPERTS = 32
TOP_K = 4
SWIGLU_LIMIT = 7.0
SWIGLU_ALPHA = 1.702
EPS = 1e-6
NEG = -1e30
ATTN_SCALE = HEAD_DIM ** -0.5
LOG2E = 1.4426950408889634

LANES = 128
VMEM_LIMIT_BYTES = 56 * 1024 * 1024

SWA_Q_W = SWA_Q_HEADS * HEAD_DIM
SWA_KV_W = SWA_KV_HEADS * HEAD_DIM
FOX_W = FOX_HEADS * HEAD_DIM
FOX_AUG_W = FOX_HEADS * LANES

PROJ_TM = 1024
SWA_TQ = 512
FOX_TQ = 1024
FOX_SUB = 512
FOX_TK = 512
MERGE_TM = 512
DISPATCH_T = 256
EXPERT_TM = 512
EXPERT_FC = 256
EXPERT_OUT_C = 256
COMBINE_T = 256

BF16 = jnp.bfloat16
F32 = jnp.float32


def _params(sem):
    return pltpu.CompilerParams(dimension_semantics=sem, vmem_limit_bytes=VMEM_LIMIT_BYTES)


def _dot(a, b):
    return jnp.dot(a, b, preferred_element_type=F32)


def _dot_nt(a, b):
    return lax.dot_general(a, b, (((1,), (1,)), ((), ())), preferred_element_type=F32)


def _split3(v):
    hi = v.astype(BF16)
    r1 = v - hi.astype(F32)
    mid = r1.astype(BF16)
    lo = (r1 - mid.astype(F32)).astype(BF16)
    return hi, mid, lo


def _sigmoid(z):
    return 0.5 * jnp.tanh(0.5 * z) + 0.5


def _pack3(v):
    hi, mid, lo = _split3(v)
    packed = hi.astype(F32) + pltpu.roll(mid.astype(F32), FOX_HEADS, 1) + pltpu.roll(lo.astype(F32), 2 * FOX_HEADS, 1)
    return packed.astype(BF16)


def _lane_iota(shape):
    return lax.broadcasted_iota(jnp.int32, shape, len(shape) - 1)


def _head_norm(t, gain, bd):
    ss = _dot((t * t).astype(BF16), bd)
    return t * lax.rsqrt(ss * (1.0 / HEAD_DIM) + EPS) * gain


def _rope(t, cos, sin_signed, first_half):
    partner = jnp.where(first_half, pltpu.roll(t, LANES - HEAD_DIM // 2, 1), pltpu.roll(t, HEAD_DIM // 2, 1))
    return t * cos + partner * sin_signed


def _in_proj_kernel(tiles_per_batch, x_ref, pos_ref, g1_ref, wqkv_ref, wfl_ref, wgl_ref, bfl_ref, bgl_ref,
                    gqa_ref, gka_ref, gqf_ref, gkf_ref, invf_ref, eq_ref, ek_ref, oneq_ref, onek_ref,
                    qa_ref, ka_ref, va_ref, qf_ref, kf_ref, vf_ref, gate_ref, carry_ref):
    i = pl.program_id(0)
    tm = x_ref.shape[0]

    x = x_ref[...]
    h = x * lax.rsqrt(jnp.mean(x * x, axis=-1, keepdims=True) + EPS) * g1_ref[...]
    h = h.astype(BF16)

    lane = _lane_iota((tm, LANES))
    in_low_head = lane < HEAD_DIM
    first_half = (lane & (HEAD_DIM - 1)) < (HEAD_DIM // 2)
    shift = HEAD_DIM.bit_length() - 1
    rr = lax.broadcasted_iota(jnp.int32, (2 * LANES, 2 * LANES), 0) >> shift
    cc = lax.broadcasted_iota(jnp.int32, (2 * LANES, 2 * LANES), 1) >> shift
    bd2 = (rr == cc).astype(BF16)
    bd = bd2[:LANES, :LANES]

    ang = pos_ref[...].astype(F32) * invf_ref[...]
    cos = jnp.cos(ang)
    sin_signed = jnp.where(first_half, -jnp.sin(ang), jnp.sin(ang))

    fl = _dot(h, wfl_ref[...]) + bfl_ref[...]
    logf = jnp.minimum(fl, 0.0) - jnp.log(1.0 + jnp.exp(-jnp.abs(fl)))
    logf = jnp.where(lane < FOX_HEADS, logf, 0.0)
    row = lax.broadcasted_iota(jnp.int32, (tm, tm), 0)
    col = lax.broadcasted_iota(jnp.int32, (tm, tm), 1)
    tri = (col <= row).astype(BF16)
    c3 = _dot(tri, _pack3(logf))
    csum = c3 + pltpu.roll(c3, LANES - FOX_HEADS, 1) + pltpu.roll(c3, LANES - 2 * FOX_HEADS, 1)
    csum = jnp.where(lane < FOX_HEADS, csum, 0.0)

    @pl.when(i % tiles_per_batch == 0)
    def _():
        carry_ref[...] = jnp.zeros_like(carry_ref)

    csum = csum + carry_ref[0:1, :]
    carry_ref[0:1, :] = csum[tm - 1:tm, :]
    c_parts = _pack3(csum * LOG2E)

    wide = 2 * LANES
    for c in range(SWA_Q_W // wide):
        t = _dot(h, wqkv_ref[:, c * wide:(c + 1) * wide])
        t = _head_norm(t, gqa_ref[:, c * wide:(c + 1) * wide], bd2)
        for g in range(2):
            lo = c * wide + g * LANES
            qa_ref[:, lo:lo + LANES] = _rope(t[:, g * LANES:(g + 1) * LANES], cos, sin_signed, first_half).astype(BF16)
    off = SWA_Q_W
    t = _dot(h, wqkv_ref[:, off:off + LANES])
    t = _rope(_head_norm(t, gka_ref[...], bd), cos, sin_signed, first_half)
    ka_ref[:, 0:LANES] = t.astype(BF16)
    ka_ref[:, LANES:2 * LANES] = pltpu.roll(t, HEAD_DIM, 1).astype(BF16)
    off += SWA_KV_W
    t = _dot(h, wqkv_ref[:, off:off + LANES])
    va_ref[:, 0:LANES] = t.astype(BF16)
    va_ref[:, LANES:2 * LANES] = pltpu.roll(t, HEAD_DIM, 1).astype(BF16)
    off += SWA_KV_W

    aug_q = _dot(c_parts, eq_ref[...]) + oneq_ref[...]
    aug_k = _dot(c_parts, ek_ref[...]) + onek_ref[...]
    for name_off, gain_ref, aug, out_ref in ((off, gqf_ref, aug_q, qf_ref), (off + FOX_W, gkf_ref, aug_k, kf_ref)):
        for c in range(FOX_W // wide):
            t2 = _dot(h, wqkv_ref[:, name_off + c * wide:name_off + (c + 1) * wide])
            t2 = _head_norm(t2, gain_ref[:, c * wide:(c + 1) * wide], bd2)
            for g in range(2):
                t = t2[:, g * LANES:(g + 1) * LANES]
                even = 2 * (2 * c + g) * LANES
                odd = even + LANES
                out_ref[:, even:even + LANES] = jnp.where(in_low_head, t, aug[:, even:even + LANES]).astype(BF16)
                out_ref[:, odd:odd + LANES] = jnp.where(in_low_head, aug[:, odd:odd + LANES], t).astype(BF16)
    off += 2 * FOX_W
    for c in range(FOX_W // LANES):
        t = _dot(h, wqkv_ref[:, off + c * LANES:off + (c + 1) * LANES])
        vf_ref[:, 2 * c * LANES:(2 * c + 1) * LANES] = jnp.where(in_low_head, t, 1.0).astype(BF16)
        vf_ref[:, (2 * c + 1) * LANES:(2 * c + 2) * LANES] = jnp.where(in_low_head, 1.0, t).astype(BF16)

    gw = 512
    for c in range(gate_ref.shape[1] // gw):
        z = _dot(h, wgl_ref[:, c * gw:(c + 1) * gw]) + bgl_ref[:, c * gw:(c + 1) * gw]
        gate_ref[:, c * gw:(c + 1) * gw] = _sigmoid(z).astype(BF16)


def _aug_placement():
    eq = np.zeros((LANES, FOX_AUG_W), np.float32)
    ek = np.zeros((LANES, FOX_AUG_W), np.float32)
    oneq = np.zeros((1, FOX_AUG_W), np.float32)
    onek = np.zeros((1, FOX_AUG_W), np.float32)
    for h in range(FOX_HEADS):
        base = h * LANES + (HEAD_DIM if h % 2 == 0 else 0)
        for p in range(3):
            eq[p * FOX_HEADS + h, base + p] = 1.0
            ek[p * FOX_HEADS + h, base + 3 + p] = -1.0
            oneq[0, base + 3 + p] = 1.0
            onek[0, base + p] = 1.0
    return jnp.asarray(eq, BF16), jnp.asarray(ek, BF16), jnp.asarray(oneq), jnp.asarray(onek)


def _in_proj(x2, pos2, norm1_g, w_in, b_forget, b_gate, q_norm_swa, k_norm_swa, q_norm_fox, k_norm_fox, seq):
    n, d = x2.shape
    tm = PROJ_TM
    n_qkv = SWA_Q_W + 2 * SWA_KV_W + 3 * FOX_W
    wqkv = w_in[:, :n_qkv].astype(BF16)
    wfl = jnp.pad(w_in[:, n_qkv:n_qkv + FOX_HEADS], ((0, 0), (0, LANES - FOX_HEADS))).astype(BF16)
    wgl = w_in[:, n_qkv + FOX_HEADS:].astype(BF16)
    n_gate = wgl.shape[1]
    bfl = jnp.pad(b_forget, (0, LANES - FOX_HEADS)).reshape(1, LANES)
    bgl = b_gate.reshape(1, n_gate)
    gqa = (jnp.tile(q_norm_swa, SWA_Q_HEADS) * ATTN_SCALE).reshape(1, SWA_Q_W)
    gka = jnp.tile(k_norm_swa, SWA_KV_HEADS).reshape(1, SWA_KV_W)
    gqf = (jnp.tile(q_norm_fox, FOX_HEADS) * (ATTN_SCALE * LOG2E)).reshape(1, FOX_W)
    gkf = jnp.tile(k_norm_fox, FOX_HEADS).reshape(1, FOX_W)
    half = HEAD_DIM // 2
    inv = ROPE_THETA ** (-jnp.arange(half, dtype=F32) / half)
    invf = jnp.tile(inv, LANES // half).reshape(1, LANES)
    eq, ek, oneq, onek = _aug_placement()

    row = lambda w: pl.BlockSpec((tm, w), lambda i: (i, 0))
    full = lambda a: pl.BlockSpec(a.shape, lambda i: (0,) * a.ndim, pipeline_mode=pl.Buffered(1))
    consts = (norm1_g.reshape(1, d), wqkv, wfl, wgl, bfl, bgl, gqa, gka, gqf, gkf, invf, eq, ek, oneq, onek)
    out_widths = (SWA_Q_W, 2 * SWA_KV_W, 2 * SWA_KV_W, FOX_AUG_W, FOX_AUG_W, 2 * FOX_W, n_gate)
    return pl.pallas_call(
        functools.partial(_in_proj_kernel, seq // tm),
        grid=(n // tm,),
        in_specs=[row(d), row(1)] + [full(a) for a in consts],
        out_specs=[row(w) for w in out_widths],
        out_shape=[jax.ShapeDtypeStruct((n, w), BF16) for w in out_widths],
        scratch_shapes=[pltpu.VMEM((8, LANES), F32)],
        compiler_params=_params(("arbitrary",)),
        name="in_proj",
    )(x2, pos2, *consts)


def _swa_kernel(sinks_ref, q_ref, kc_ref, kp_ref, vc_ref, vp_ref, o_ref):
    i = pl.program_id(1)
    tq = q_ref.shape[1]
    blk = SWA_BLOCK
    lane = _lane_iota((2 * blk, LANES))
    low = lane < HEAD_DIM
    qi = lax.broadcasted_iota(jnp.int32, (blk, 2 * blk), 0)
    kj = lax.broadcasted_iota(jnp.int32, (blk, 2 * blk), 1)
    cur_ok = (kj >= blk) & (kj - blk <= qi)
    prev_ok = (kj < blk) & (kj > qi)
    zero = jnp.zeros((2 * blk, LANES), BF16)
    group = SWA_Q_HEADS // SWA_KV_HEADS

    for j in range(tq // blk):
        if j == 0:
            k_prev, v_prev = kp_ref[0], vp_ref[0]
            valid = cur_ok | (prev_ok & (i > 0))
        else:
            k_prev, v_prev = kc_ref[0, (j - 1) * blk:j * blk, :], vc_ref[0, (j - 1) * blk:j * blk, :]
            valid = cur_ok | prev_ok
        kwin = jnp.concatenate([k_prev, kc_ref[0, j * blk:(j + 1) * blk, :]], axis=0)
        vwin = jnp.concatenate([v_prev, vc_ref[0, j * blk:(j + 1) * blk, :]], axis=0)
        nat_k, swp_k = kwin[:, :LANES], kwin[:, LANES:]
        nat_v, swp_v = vwin[:, :LANES], vwin[:, LANES:]
        k_var = ((jnp.where(low, nat_k, zero), jnp.where(low, zero, swp_k)),
                 (jnp.where(low, swp_k, zero), jnp.where(low, zero, nat_k)))
        v_var = ((jnp.where(low, nat_v, zero), jnp.where(low, zero, swp_v)),
                 (jnp.where(low, swp_v, zero), jnp.where(low, zero, nat_v)))
        for c in range(SWA_Q_W // LANES):
            qc = q_ref[0, j * blk:(j + 1) * blk, c * LANES:(c + 1) * LANES]
            out = jnp.zeros((blk, LANES), F32)
            for half in range(2):
                head = 2 * c + half
                kv = head // group
                s = jnp.where(valid, _dot_nt(qc, k_var[kv][half]), NEG)
                sink = sinks_ref[head]
                m = jnp.maximum(jnp.max(s, axis=-1, keepdims=True), sink)
                p = jnp.exp(s - m)
                den = jnp.sum(p, axis=-1, keepdims=True) + jnp.exp(sink - m)
                out = out + _dot(p.astype(BF16), v_var[kv][half]) / den
            o_ref[0, j * blk:(j + 1) * blk, c * LANES:(c + 1) * LANES] = out.astype(BF16)


def _swa(qa, ka2, va2, sinks):
    b, s, _ = qa.shape
    tq = SWA_TQ
    per = tq // SWA_BLOCK
    cur = lambda w: pl.BlockSpec((1, tq, w), lambda bi, i, sk: (bi, i, 0))
    prev = lambda w: pl.BlockSpec((1, SWA_BLOCK, w), lambda bi, i, sk: (bi, jnp.maximum(i * per - 1, 0), 0))
    return pl.pallas_call(
        _swa_kernel,
        grid_spec=pltpu.PrefetchScalarGridSpec(
            num_scalar_prefetch=1,
            grid=(b, s // tq),
            in_specs=[cur(SWA_Q_W), cur(2 * SWA_KV_W), prev(2 * SWA_KV_W), cur(2 * SWA_KV_W), prev(2 * SWA_KV_W)],
            out_specs=cur(SWA_Q_W),
        ),
        out_shape=jax.ShapeDtypeStruct((b, s, SWA_Q_W), BF16),
        compiler_params=_params(("arbitrary", "arbitrary")),
        name="swa",
    )(sinks, qa, ka2, ka2, va2, va2)


def _fox_kernel(q_ref, k_ref, v_ref, o_ref, m_ref, acc_ref, s_ref):
    qi = pl.program_id(2)
    sub, tk = FOX_SUB, FOX_TK
    tq = q_ref.shape[1]
    n_sub = tq // sub
    n_chunk = tk // LANES
    row = lax.broadcasted_iota(jnp.int32, (sub, LANES), 0)
    col = lax.broadcasted_iota(jnp.int32, (sub, LANES), 1)
    chains = [(half, sb) for sb in range(n_sub) for half in range(2)]

    m_ref[...] = jnp.full_like(m_ref, NEG)
    acc_ref[...] = jnp.zeros_like(acc_ref)

    def scores(buf, chain, kb):
        half, sb = chains[chain]
        start = pl.multiple_of(kb * tk, tk)
        q = q_ref[0, sb * sub:(sb + 1) * sub, half * LANES:(half + 1) * LANES]
        k = k_ref[0, pl.ds(start, tk), half * LANES:(half + 1) * LANES]
        s_ref[buf, chain] = _dot_nt(q, k)

    def fold(buf, chain, kb, mask_offset):
        half, sb = chains[chain]
        start = pl.multiple_of(kb * tk, tk)
        v = v_ref[0, pl.ds(start, tk), half * LANES:(half + 1) * LANES]
        s = [s_ref[buf, chain, :, c * LANES:(c + 1) * LANES] for c in range(n_chunk)]
        if mask_offset is not None:
            s = [jnp.where(col + (c * LANES + mask_offset) <= row, s[c], NEG) for c in range(n_chunk)]
        m_prev = m_ref[chain]
        peak = functools.reduce(jnp.maximum, s)
        m_new = jnp.maximum(m_prev, jnp.max(peak, axis=-1, keepdims=True))
        alpha = jnp.exp2(m_prev - m_new)
        p = [jnp.exp2(sc - m_new) for sc in s]
        pv = _dot(jnp.concatenate(p, axis=1).astype(BF16), v)
        acc_ref[chain] = alpha * acc_ref[chain] + pv
        m_ref[chain] = m_new

    assert tq // tk == 2
    every = range(len(chains))
    for chain in every:
        scores(0, chain, 0)

    def body(t, carry):
        for chain in every:
            scores(1, chain, 2 * t + 1)
        for chain in every:
            fold(0, chain, 2 * t, None)
        for chain in every:
            scores(0, chain, 2 * t + 2)
        for chain in every:
            fold(1, chain, 2 * t + 1, None)
        return carry

    lax.fori_loop(0, qi, body, 0)

    def visibility(j, sb):
        first_key, first_row = j * tk, sb * sub
        if first_key + tk - 1 <= first_row:
            return True, None
        if first_key <= first_row + sub - 1:
            return True, first_key - first_row
        return False, None

    for chain, (half, sb) in enumerate(chains):
        if visibility(1, sb)[0]:
            scores(1, chain, 2 * qi + 1)
    for j in range(2):
        for chain, (half, sb) in enumerate(chains):
            visible, mask_offset = visibility(j, sb)
            if visible:
                fold(j, chain, 2 * qi + j, mask_offset)

    low = _lane_iota((sub, LANES)) < HEAD_DIM
    for sb in range(n_sub):
        even, odd = acc_ref[2 * sb], acc_ref[2 * sb + 1]
        even = even / pltpu.roll(even, HEAD_DIM, 1)
        odd = odd / pltpu.roll(odd, HEAD_DIM, 1)
        o_ref[0, sb * sub:(sb + 1) * sub, :] = jnp.where(low, even, odd).astype(BF16)


def _fox(qf, kf, vf):
    b, s, _ = qf.shape
    tq = FOX_TQ
    pairs = FOX_HEADS // 2
    n_chain = 2 * (tq // FOX_SUB)
    return pl.pallas_call(
        _fox_kernel,
        grid=(b, pairs, s // tq),
        in_specs=[
            pl.BlockSpec((1, tq, 2 * LANES), lambda bi, p, i: (bi, i, p)),
            pl.BlockSpec((1, s, 2 * LANES), lambda bi, p, i: (bi, 0, p)),
            pl.BlockSpec((1, s, 2 * LANES), lambda bi, p, i: (bi, 0, p)),
        ],
        out_specs=pl.BlockSpec((1, tq, LANES), lambda bi, p, i: (bi, i, p)),
        out_shape=jax.ShapeDtypeStruct((b, s, FOX_W), BF16),
        scratch_shapes=[pltpu.VMEM((n_chain, FOX_SUB, LANES), F32)] * 2
        + [pltpu.VMEM((2, n_chain, FOX_SUB, FOX_TK), F32)],
        compiler_params=_params(("arbitrary", "arbitrary", "arbitrary")),
        name="fox",
    )(qf, kf, vf)


ROW_SUB = 8


def _group(ref, j, rows, base=0):
    return (pl.ds(base * ROW_SUB + j, rows, stride=ROW_SUB), slice(None))


def _store_row_tiles(ref, v):
    for j in range(ROW_SUB):
        ref[_group(ref, j, v.shape[0])] = v[:, j * LANES:(j + 1) * LANES]


def _load_row_tiles(ref):
    rows = ref.shape[0] // ROW_SUB
    return jnp.concatenate([ref[_group(ref, j, rows)] for j in range(ROW_SUB)], axis=1)


def _tile_at(ref, first_sublane):
    return ref.at[pl.ds(pl.multiple_of(first_sublane, ROW_SUB), ROW_SUB)]


def _row_tile(ref, r):
    return _tile_at(ref, r * ROW_SUB)


def _merge_kernel(oa_ref, ob_ref, gate_ref, x_ref, woa_ref, wob_ref, wout_ref, g2_ref, wr2_ref, br_ref,
                  x1_ref, h2_ref, idx_ref, gw_ref, rank_ref, cnt_ref, carry_ref):
    i = pl.program_id(0)
    tm, d = x_ref.shape

    ga = gate_ref[:, :d].astype(F32)
    gb = gate_ref[:, d:].astype(F32)
    merged = ga * _dot(oa_ref[...], woa_ref[...]) + gb * _dot(ob_ref[...], wob_ref[...])
    x1 = x_ref[...] + _dot(merged.astype(BF16), wout_ref[...])
    x1_ref[...] = x1
    h2 = x1 * lax.rsqrt(jnp.mean(x1 * x1, axis=-1, keepdims=True) + EPS) * g2_ref[...]
    _store_row_tiles(h2_ref, h2)

    h_hi = h2.astype(BF16)
    h_lo = (h2 - h_hi.astype(F32)).astype(BF16)
    both = _dot(h_hi, wr2_ref[...])
    logits = both[:, :LANES] + both[:, LANES:] + _dot(h_lo, wr2_ref[:, :LANES]) + br_ref[...]
    lane = _lane_iota((tm, LANES))
    lanef = lane.astype(F32)
    logits = jnp.where(lane < N_EXPERTS, logits, NEG)

    sels, vals = [], []
    for _ in range(TOP_K):
        mx = jnp.max(logits, axis=-1, keepdims=True)
        ix = jnp.min(jnp.where(logits == mx, lanef, float(LANES)), axis=-1, keepdims=True)
        sel = lanef == ix
        sels.append((sel, ix))
        vals.append(mx)
        logits = jnp.where(sel, 3.0 * NEG, logits)
    exps = [jnp.exp(v - vals[0]) for v in vals]
    den = exps[0] + exps[1] + exps[2] + exps[3]

    @pl.when(i == 0)
    def _():
        carry_ref[...] = jnp.zeros_like(carry_ref)

    onehot = jnp.zeros((tm, LANES), F32)
    for sel, _ in sels:
        onehot = onehot + sel.astype(F32)
    row = lax.broadcasted_iota(jnp.int32, (tm, tm), 0)
    col = lax.broadcasted_iota(jnp.int32, (tm, tm), 1)
    before = _dot((col < row).astype(BF16), onehot.astype(BF16)) + carry_ref[0:1, :]
    total = carry_ref[0:1, :] + jnp.sum(onehot, axis=0, keepdims=True)
    carry_ref[0:1, :] = total
    cnt_ref[...] = jnp.broadcast_to(total, cnt_ref.shape)

    idx_out = jnp.zeros((tm, LANES), F32)
    gw_out = jnp.zeros((tm, LANES), F32)
    rank_out = jnp.zeros((tm, LANES), F32)
    for k, (sel, ix) in enumerate(sels):
        slot = lane == k
        rank = jnp.sum(jnp.where(sel, before, 0.0), axis=-1, keepdims=True)
        idx_out = jnp.where(slot, ix, idx_out)
        gw_out = jnp.where(slot, exps[k] / den, gw_out)
        rank_out = jnp.where(slot, rank, rank_out)
    idx_ref[...] = idx_out.astype(jnp.int32)
    gw_ref[...] = gw_out
    rank_ref[...] = rank_out.astype(jnp.int32)


def _merge(out_a, out_b, gates, x2, w_o_swa, w_o_fox, w_out, norm2_g, w_router, b_router):
    n, d = x2.shape
    tm = MERGE_TM
    wr = jnp.pad(w_router, ((0, 0), (0, LANES - N_EXPERTS)))
    wrh = wr.astype(BF16)
    wrl = (wr - wrh.astype(F32)).astype(BF16)
    wr2 = jnp.concatenate([wrh, wrl], axis=1)
    br = jnp.pad(b_router, (0, LANES - N_EXPERTS)).reshape(1, LANES)
    consts = (w_o_swa.astype(BF16), w_o_fox.astype(BF16), w_out.astype(BF16), norm2_g.reshape(1, d), wr2, br)
    row = lambda w: pl.BlockSpec((tm, w), lambda i: (i, 0))
    full = lambda a: pl.BlockSpec(a.shape, lambda i: (0,) * a.ndim)
    return pl.pallas_call(
        _merge_kernel,
        grid=(n // tm,),
        in_specs=[row(out_a.shape[1]), row(out_b.shape[1]), row(gates.shape[1]), row(d)] + [full(a) for a in consts],
        out_specs=[row(d), pl.BlockSpec((tm * ROW_SUB, LANES), lambda i: (i, 0)), row(LANES), row(LANES), row(LANES),
                   pl.BlockSpec((8, LANES), lambda i: (0, 0))],
        out_shape=[
            jax.ShapeDtypeStruct((n, d), F32),
            jax.ShapeDtypeStruct((n * ROW_SUB, LANES), F32),
            jax.ShapeDtypeStruct((n, LANES), jnp.int32),
            jax.ShapeDtypeStruct((n, LANES), F32),
            jax.ShapeDtypeStruct((n, LANES), jnp.int32),
            jax.ShapeDtypeStruct((8, LANES), F32),
        ],
        scratch_shapes=[pltpu.VMEM((8, LANES), F32)],
        compiler_params=_params(("arbitrary",)),
        name="merge",
    )(out_a, out_b, gates, x2, *consts)


def _idx_copy(dest_ref, idx_ref, sem_ref, tile, slot, per_tile):
    src = dest_ref.at[pl.ds(pl.multiple_of(tile * per_tile, per_tile), per_tile)]
    dst = idx_ref.at[pl.ds(pl.multiple_of(slot * per_tile, per_tile), per_tile)]
    return pltpu.make_async_copy(src, dst, sem_ref.at[slot])


def _dispatch_kernel(tail_start_ref, tail_flag_ref, n_used_ref, dest_ref, h_ref, xin_ref, idx_ref, zeros_ref, idx_sem,
                     row_sem, zero_sem):
    i = pl.program_id(0)
    n_tiles = pl.num_programs(0)
    t = h_ref.shape[0] // ROW_SUB
    per_tile = t * TOP_K
    slot = i % 2
    tm = zeros_ref.shape[0] // ROW_SUB
    n_blocks = xin_ref.shape[0] // zeros_ref.shape[0]

    def zero_copy(start):
        start = pl.multiple_of(start * ROW_SUB, ROW_SUB)
        return pltpu.make_async_copy(zeros_ref, xin_ref.at[pl.ds(start, tm * ROW_SUB)], zero_sem)

    def unused_block(op):
        def body(blk, carry):
            op(zero_copy(blk * tm))
            return carry
        lax.fori_loop(n_used_ref[0], n_blocks, body, 0)

    @pl.when(i == 0)
    def _():
        zeros_ref[...] = jnp.zeros_like(zeros_ref)
        _idx_copy(dest_ref, idx_ref, idx_sem, 0, 0, per_tile).start()
        for e in range(N_EXPERTS):
            @pl.when(tail_flag_ref[e] > 0)
            def _():
                zero_copy(tail_start_ref[e]).start()
        unused_block(lambda c: c.start())
        for e in range(N_EXPERTS):
            @pl.when(tail_flag_ref[e] > 0)
            def _():
                zero_copy(tail_start_ref[e]).wait()
        unused_block(lambda c: c.wait())

    @pl.when(i + 1 < n_tiles)
    def _():
        _idx_copy(dest_ref, idx_ref, idx_sem, i + 1, 1 - slot, per_tile).start()

    _idx_copy(dest_ref, idx_ref, idx_sem, i, slot, per_tile).wait()

    base = slot * per_tile

    def row_copy(r, k):
        return pltpu.make_async_copy(_row_tile(h_ref, r), _tile_at(xin_ref, idx_ref[base + r * TOP_K + k]), row_sem)

    def issue(r, carry):
        for k in range(TOP_K):
            row_copy(r, k).start(priority=k % 2)
        return carry

    def drain(r, carry):
        for k in range(TOP_K):
            row_copy(r, k).wait()
        return carry

    lax.fori_loop(0, t, issue, 0, unroll=8)
    lax.fori_loop(0, t, drain, 0, unroll=8)


def _dispatch(dest, h2t, tail_start, tail_flag, n_used, cap):
    n = h2t.shape[0] // ROW_SUB
    t = DISPATCH_T
    return pl.pallas_call(
        _dispatch_kernel,
        grid_spec=pltpu.PrefetchScalarGridSpec(
            num_scalar_prefetch=3,
            grid=(n // t,),
            in_specs=[pl.BlockSpec(memory_space=pl.ANY), pl.BlockSpec((t * ROW_SUB, LANES), lambda i, *_: (i, 0))],
            out_specs=pl.BlockSpec(memory_space=pl.ANY),
            scratch_shapes=[
                pltpu.SMEM((2 * t * TOP_K,), jnp.int32),
                pltpu.VMEM((EXPERT_TM * ROW_SUB, LANES), F32),
                pltpu.SemaphoreType.DMA((2,)),
                pltpu.SemaphoreType.DMA,
                pltpu.SemaphoreType.DMA,
            ],
        ),
        out_shape=jax.ShapeDtypeStruct((cap * ROW_SUB, LANES), F32),
        compiler_params=_params(("arbitrary",)),
        name="dispatch",
    )(tail_start, tail_flag, n_used, dest, h2t)


def _expert_kernel(blk_expert_ref, n_used_ref, x_ref, wgu_ref, bgu_ref, wd_ref, bd_ref, y_ref, wgu_bf, wd_bf):
    i = pl.program_id(0)
    f = wd_ref.shape[1]
    used = i < n_used_ref[0]

    new_expert = (i == 0) | (blk_expert_ref[i] != blk_expert_ref[jnp.maximum(i - 1, 0)])

    @pl.when(used & new_expert)
    def _():
        step = EXPERT_FC
        for r in range(0, wgu_ref.shape[1], step):
            wgu_bf[r:r + step, :] = wgu_ref[0, r:r + step, :].astype(BF16)
        for r in range(0, f, step):
            wd_bf[r:r + step, :] = wd_ref[0, r:r + step, :].astype(BF16)

    @pl.when(used)
    def _():
        x = _load_row_tiles(x_ref).astype(BF16)
        g = _dot(x, wgu_bf[:, :f]) + bgu_ref[0, :, :f]
        u = _dot(x, wgu_bf[:, f:]) + bgu_ref[0, :, f:]
        g = jnp.minimum(g, SWIGLU_LIMIT)
        u = jnp.clip(u, -SWIGLU_LIMIT, SWIGLU_LIMIT)
        a = ((u + 1.0) * (g * _sigmoid(SWIGLU_ALPHA * g))).astype(BF16)
        rows = a.shape[0]
        for c in range(wd_bf.shape[1] // EXPERT_OUT_C):
            lo = c * EXPERT_OUT_C
            yc = _dot(a, wd_bf[:, lo:lo + EXPERT_OUT_C]) + bd_ref[0, :, lo:lo + EXPERT_OUT_C]
            for q in range(EXPERT_OUT_C // LANES):
                y_ref[_group(y_ref, lo // LANES + q, rows)] = yc[:, q * LANES:(q + 1) * LANES]

    @pl.when(i >= n_used_ref[0])
    def _():
        y_ref[...] = jnp.zeros_like(y_ref)


def _experts(xin, blk_expert, n_used, w_gate_up, b_gate_up, w_down, b_down):
    cap = xin.shape[0] // ROW_SUB
    tm = EXPERT_TM
    ne, d, f2 = w_gate_up.shape
    f = f2 // 2
    rows = lambda i, be, nu: (jnp.minimum(i, nu[0] - 1), 0)
    by_expert = lambda i, be, nu: (be[i], 0, 0)
    return pl.pallas_call(
        _expert_kernel,
        grid_spec=pltpu.PrefetchScalarGridSpec(
            num_scalar_prefetch=2,
            grid=(cap // tm,),
            in_specs=[
                pl.BlockSpec((tm * ROW_SUB, LANES), rows),
                pl.BlockSpec((1, d, f2), by_expert),
                pl.BlockSpec((1, 1, f2), by_expert),
                pl.BlockSpec((1, f, d), by_expert),
                pl.BlockSpec((1, 1, d), by_expert),
            ],
            out_specs=pl.BlockSpec((tm * ROW_SUB, LANES), lambda i, be, nu: (i, 0)),
            scratch_shapes=[pltpu.VMEM((d, f2), BF16), pltpu.VMEM((f, d), BF16)],
        ),
        out_shape=jax.ShapeDtypeStruct((cap * ROW_SUB, LANES), F32),
        compiler_params=_params(("arbitrary",)),
        name="experts",
    )(blk_expert, n_used, xin, w_gate_up, b_gate_up.reshape(ne, 1, f2), w_down, b_down.reshape(ne, 1, d))


def _combine_kernel(dest_ref, y_ref, x1_ref, gw_ref, o_ref, idx_ref, rows_ref, idx_sem, row_sem):
    i = pl.program_id(0)
    n_tiles = pl.num_programs(0)
    t = x1_ref.shape[0]
    per_tile = t * TOP_K
    slot = i % 2

    def row_copy(tile_slot, r, k, d):
        dst = _row_tile(rows_ref, (tile_slot * TOP_K + k) * t + r)
        return pltpu.make_async_copy(_tile_at(y_ref, d), dst, row_sem.at[tile_slot])

    def gather(tile_slot):
        base = tile_slot * per_tile

        def issue(r, carry):
            for k in range(TOP_K):
                row_copy(tile_slot, r, k, idx_ref[base + r * TOP_K + k]).start(priority=k % 2)
            return carry

        lax.fori_loop(0, t, issue, 0, unroll=8)

    @pl.when(i == 0)
    def _():
        _idx_copy(dest_ref, idx_ref, idx_sem, 0, 0, per_tile).start()
        _idx_copy(dest_ref, idx_ref, idx_sem, 0, 0, per_tile).wait()
        gather(0)

        @pl.when(n_tiles > 1)
        def _():
            _idx_copy(dest_ref, idx_ref, idx_sem, 1, 1, per_tile).start()

    @pl.when(i + 1 < n_tiles)
    def _():
        _idx_copy(dest_ref, idx_ref, idx_sem, i + 1, 1 - slot, per_tile).wait()
        gather(1 - slot)

    @pl.when(i + 2 < n_tiles)
    def _():
        _idx_copy(dest_ref, idx_ref, idx_sem, i + 2, slot, per_tile).start()

    def drain(r, carry):
        for k in range(TOP_K):
            row_copy(slot, r, k, 0).wait()
        return carry

    lax.fori_loop(0, t, drain, 0, unroll=8)

    weights = [jnp.broadcast_to(gw_ref[:, k:k + 1], (t, LANES)) for k in range(TOP_K)]
    for j in range(ROW_SUB):
        acc = x1_ref[:, j * LANES:(j + 1) * LANES]
        for k in range(TOP_K):
            acc = acc + weights[k] * rows_ref[_group(rows_ref, j, t, base=(slot * TOP_K + k) * t)]
        o_ref[:, j * LANES:(j + 1) * LANES] = acc


def _combine(dest, yb, x1, gw):
    n, d = x1.shape
    t = COMBINE_T
    return pl.pallas_call(
        _combine_kernel,
        grid=(n // t,),
        in_specs=[
            pl.BlockSpec(memory_space=pl.ANY),
            pl.BlockSpec(memory_space=pl.ANY),
            pl.BlockSpec((t, d), lambda i: (i, 0)),
            pl.BlockSpec((t, LANES), lambda i: (i, 0)),
        ],
        out_specs=pl.BlockSpec((t, d), lambda i: (i, 0)),
        out_shape=jax.ShapeDtypeStruct((n, d), F32),
        scratch_shapes=[
            pltpu.SMEM((2 * t * TOP_K,), jnp.int32),
            pltpu.VMEM((2 * TOP_K * t * ROW_SUB, LANES), F32),
            pltpu.SemaphoreType.DMA((2,)),
            pltpu.SemaphoreType.DMA((2,)),
        ],
        compiler_params=_params(("arbitrary",)),
        name="combine",
    )(dest, yb, x1, gw)


def _layer(x, positions, norm1_g, w_in, b_forget, b_gate, q_norm_swa, k_norm_swa, sinks, q_norm_fox, k_norm_fox,
           w_o_swa, w_o_fox, w_out, norm2_g, w_router, b_router, w_gate_up, b_gate_up, w_down, b_down):
    b, s, d = x.shape
    n = b * s
    assert s % max(PROJ_TM, SWA_TQ, FOX_TQ) == 0 and n % max(MERGE_TM, DISPATCH_T, COMBINE_T) == 0
    assert d == ROW_SUB * LANES
    x2 = x.reshape(n, d)

    qa, ka2, va2, qf, kf, vf, gates = _in_proj(x2, positions.reshape(n, 1), norm1_g, w_in, b_forget, b_gate,
                                               q_norm_swa, k_norm_swa, q_norm_fox, k_norm_fox, s)
    seq = lambda a: a.reshape(b, s, a.shape[1])
    out_a = _swa(seq(qa), seq(ka2), seq(va2), sinks).reshape(n, SWA_Q_W)
    out_b = _fox(seq(qf), seq(kf), seq(vf)).reshape(n, FOX_W)

    x1, h2t, idx, gw, rank, totals = _merge(out_a, out_b, gates, x2, w_o_swa, w_o_fox, w_out, norm2_g, w_router,
                                            b_router)

    tm = EXPERT_TM
    cap = n * TOP_K + N_EXPERTS * tm
    counts = totals[0, :N_EXPERTS].astype(jnp.int32)
    padded = (counts + tm - 1) // tm * tm
    pad_end = jnp.cumsum(padded)
    pad_start = pad_end - padded
    dest = ((pad_start[idx[:, :TOP_K]] + rank[:, :TOP_K]) * ROW_SUB).reshape(n * TOP_K)
    blk_start = jnp.arange(cap // tm, dtype=jnp.int32) * tm
    blk_expert = jnp.minimum(jnp.sum(pad_end[None, :] <= blk_start[:, None], axis=1), N_EXPERTS - 1).astype(jnp.int32)
    n_used = (pad_end[-1:] // tm).astype(jnp.int32)

    xin = _dispatch(dest, h2t, (pad_end - tm).astype(jnp.int32), (padded > 0).astype(jnp.int32), n_used, cap)
    yb = _experts(xin, blk_expert, n_used, w_gate_up, b_gate_up, w_down, b_down)
    return _combine(dest, yb, x1, gw).reshape(b, s, d)


def kernel(x, positions, norm1_g, w_in, b_forget, b_gate, q_norm_swa, k_norm_swa, sinks, q_norm_fox, k_norm_fox,
           w_o_swa, w_o_fox, w_out, norm2_g, w_router, b_router, w_gate_up, b_gate_up, w_down, b_down):
    for l in range(norm1_g.shape[0]):
        x = _layer(x, positions, norm1_g[l], w_in[l], b_forget[l], b_gate[l], q_norm_swa[l], k_norm_swa[l],
                   sinks[l], q_norm_fox[l], k_norm_fox[l], w_o_swa[l], w_o_fox[l], w_out[l], norm2_g[l],
                   w_router[l], b_router[l], w_gate_up[l], b_gate_up[l], w_down[l], b_down[l])
    return x
```
